```python
import jax, jax.numpy as jnp
from jax import lax
import numpy as np

D_MODEL = 1024
BATCH = 2
SEQ = 8192
DEPTH = 2

GRID_W = 64
CTX_LEN = 256
N_MIXERS = 2
RET_HEADS = 4
RET_QK_DIM = D_MODEL // RET_HEADS
RET_V_DIM = 2 * D_MODEL // RET_HEADS
RET_CHUNK = 128
DECAY_EXP_FWD = 5.0
DECAY_EXP_BWD = 5.5
ROPE_BASE = 10000.0
CONV_WIDTH = 3
FFN_HIDDEN = -(-8 * D_MODEL // (3 * 256)) * 256
N_RET_LAYERS = (DEPTH + N_MIXERS - 1) // N_MIXERS
N_CONV_LAYERS = DEPTH // N_MIXERS
EPS = 1e-6

kernel_name = "hybrid_retention_shortconv_dit"


def rms_norm(x, gain=None):
    xf = x.astype(jnp.float32)
    y = xf * lax.rsqrt(jnp.mean(xf * xf, axis=-1, keepdims=True) + EPS)
    if gain is not None:
        y = y * gain.astype(jnp.float32)
    return y.astype(x.dtype)


def ada_params(cond, w, b):
    return jnp.split(jax.nn.silu(cond) @ w + b, 6, axis=-1)


def modulate(x, shift, scale):
    return x * (1 + scale) + shift


def rope_1d(x, pos):
    n = x.shape[-1] // 2
    freqs = ROPE_BASE ** (-jnp.arange(n, dtype=jnp.float32) / n)
    ang = pos.astype(jnp.float32)[:, None] * freqs[None, :]
    cos, sin = jnp.cos(ang).astype(x.dtype), jnp.sin(ang).astype(x.dtype)
    x1, x2 = x[..., :n], x[..., n:]
    return jnp.concatenate([x1 * cos - x2 * sin, x1 * sin + x2 * cos], axis=-1)


def rope_2d(x, row, col):
    half = x.shape[-1] // 2
    return jnp.concatenate([rope_1d(x[..., :half], row), rope_1d(x[..., half:], col)], axis=-1)


def retention_log_decays():
    h = jnp.arange(RET_HEADS, dtype=jnp.float32)
    lg_f = jnp.log1p(-jnp.exp2(-DECAY_EXP_FWD - h))
    lg_b = jnp.log1p(-jnp.exp2(-DECAY_EXP_BWD - h))
    return lg_f, lg_b


def retention_chunk_scan(q, k, v, log_g, s0):
    bsz, nh, L, _ = q.shape
    dv = v.shape[-1]
    n_chunks = L // RET_CHUNK

    def to_chunks(t):
        return t.reshape(bsz, nh, n_chunks, RET_CHUNK, t.shape[-1]).transpose(2, 0, 1, 3, 4)

    idx = jnp.arange(RET_CHUNK, dtype=jnp.float32)
    diff = idx[:, None] - idx[None, :]
    decay_mat = jnp.where(diff[None] >= 0,
                          jnp.exp(jnp.maximum(diff, 0.0)[None] * log_g[:, None, None]), 0.0)
    q_dec = jnp.exp((idx[None, :] + 1.0) * log_g[:, None])[None, :, :, None]
    k_dec = jnp.exp((RET_CHUNK - 1.0 - idx[None, :]) * log_g[:, None])[None, :, :, None]
    chunk_dec = jnp.exp(RET_CHUNK * log_g)[None, :, None, None]

    def step(s, inp):
        qc, kc, vc = inp
        scores = jnp.einsum('bhid,bhjd->bhij', qc, kc) * decay_mat[None]
        intra = jnp.einsum('bhij,bhje->bhie', scores, vc)
        cross = jnp.einsum('bhid,bhde->bhie', qc * q_dec, s)
        s_new = s * chunk_dec + jnp.einsum('bhjd,bhje->bhde', kc * k_dec, vc)
        return s_new, intra + cross

    _, out = lax.scan(step, s0, (to_chunks(q), to_chunks(k), to_chunks(v)))
    return out.transpose(1, 2, 0, 3, 4).reshape(bsz, nh, L, dv)


def context_state(k, v, log_g, reverse):
    L = k.shape[2]
    j = jnp.arange(L, dtype=jnp.float32)
    dist = j if reverse else (L - 1.0) - j
    w = jnp.exp(dist[None, :] * log_g[:, None])
    return jnp.einsum('bhld,bhle->bhde', k * w[None, :, :, None], v)


def context_parallel(q, k, v, lg_f, lg_b):
    L = q.shape[2]
    idx = jnp.arange(L, dtype=jnp.float32)
    diff = idx[:, None] - idx[None, :]
    d_f = jnp.where(diff[None] >= 0, jnp.exp(jnp.maximum(diff, 0.0)[None] * lg_f[:, None, None]), 0.0)
    d_b = jnp.where(diff[None] <= 0, jnp.exp(jnp.maximum(-diff, 0.0)[None] * lg_b[:, None, None]), 0.0)
    scores = jnp.einsum('bhid,bhjd->bhij', q, k) * (d_f + d_b)[None]
    return jnp.einsum('bhij,bhje->bhie', scores, v)


def retention_output(o, g, w_o):
    bsz, nh, L, dv = o.shape
    o = o * lax.rsqrt(jnp.mean(o * o, axis=-1, keepdims=True) + EPS)
    o = o.transpose(0, 2, 1, 3).reshape(bsz, L, nh * dv).astype(g.dtype)
    return (jax.nn.silu(g) * o) @ w_o


def retention_mixer(ax, ac, w_qkvg, w_o, row, col, with_ctx_out):
    D = D_MODEL
    scale = RET_QK_DIM ** -0.5

    def heads(t, d):
        return t.reshape(t.shape[0], t.shape[1], RET_HEADS, d).transpose(0, 2, 1, 3).astype(jnp.float32)

    qx, kx, vx, gx = jnp.split(ax @ w_qkvg, [D, 2 * D, 4 * D], axis=-1)
    qx = rope_2d(heads(qx, RET_QK_DIM), row, col)
    kx = rope_2d(heads(kx, RET_QK_DIM), row, col) * scale
    vx = heads(vx, RET_V_DIM)

    if with_ctx_out:
        qc, kc, vc, gc = jnp.split(ac @ w_qkvg, [D, 2 * D, 4 * D], axis=-1)
    else:
        kc, vc = jnp.split(ac @ w_qkvg[:, D:4 * D], [D], axis=-1)
    kc = heads(kc, RET_QK_DIM) * scale
    vc = heads(vc, RET_V_DIM)

    lg_f, lg_b = retention_log_decays()
    s_f = context_state(kc, vc, lg_f, reverse=False)
    s_b = context_state(kc, vc, lg_b, reverse=True)

    o_f = retention_chunk_scan(qx, kx, vx, lg_f, s_f)
    o_b = jnp.flip(retention_chunk_scan(jnp.flip(qx, 2), jnp.flip(kx, 2), jnp.flip(vx, 2), lg_b, s_b), 2)
    yx = retention_output(o_f + o_b, gx, w_o)

    yc = None
    if with_ctx_out:
        yc = retention_output(context_parallel(heads(qc, RET_QK_DIM), kc, vc, lg_f, lg_b), gc, w_o)
    return yx, yc


def short_conv(u, w):
    return lax.conv_general_dilated(
        u, w[:, None, :].astype(u.dtype), window_strides=(1,),
        padding=[(CONV_WIDTH // 2, CONV_WIDTH // 2)],
        dimension_numbers=('NWC', 'WIO', 'NWC'), feature_group_count=u.shape[-1])


def conv_mixer(h, w_in, w_conv, w_out):
    b_gate, c_gate, xv = jnp.split(h @ w_in, 3, axis=-1)
    return (b_gate * short_conv(c_gate * xv, w_conv)) @ w_out


def swiglu(h, w1, w3, w2):
    return (jax.nn.silu(h @ w1) * (h @ w3)) @ w2


def setup_inputs(seed: int = 0) -> dict:
    key = jax.random.key(seed)
    ks = jax.random.split(key, 20)
    D, F = D_MODEL, FFN_HIDDEN
    nrm = jax.random.normal
    f32 = jnp.float32
    return {
        "x": nrm(ks[0], (BATCH, SEQ, D), f32),
        "c": nrm(ks[1], (BATCH, D), f32),
        "ctx": nrm(ks[2], (BATCH, CTX_LEN, D), f32),
        "c_ctx": nrm(ks[3], (D,), f32),
        "ada_w": nrm(ks[4], (DEPTH, D, 6 * D), f32) * (0.5 * D ** -0.5),
        "ada_b": nrm(ks[5], (DEPTH, 6 * D), f32) * 0.02,
        "norm_mix": 1.0 + 0.02 * nrm(ks[6], (DEPTH, D), f32),
        "norm_ffn": 1.0 + 0.02 * nrm(ks[7], (DEPTH, D), f32),
        "ret_w_qkvg": nrm(ks[8], (N_RET_LAYERS, D, 6 * D), f32) * D ** -0.5,
        "ret_w_o": nrm(ks[9], (N_RET_LAYERS, 2 * D, D), f32) * (2 * D) ** -0.5,
        "conv_w_in": nrm(ks[10], (N_CONV_LAYERS, D, 3 * D), f32) * D ** -0.5,
        "conv_w": nrm(ks[11], (N_CONV_LAYERS, CONV_WIDTH, D), f32) * CONV_WIDTH ** -0.5,
        "conv_w_out": nrm(ks[12], (N_CONV_LAYERS, D, D), f32) * D ** -0.5,
        "ffn_w1": nrm(ks[13], (DEPTH, D, F), f32) * D ** -0.5,
        "ffn_w3": nrm(ks[14], (DEPTH, D, F), f32) * D ** -0.5,
        "ffn_w2": nrm(ks[15], (DEPTH, F, D), f32) * F ** -0.5,
        "final_norm": 1.0 + 0.02 * nrm(ks[16], (D,), f32),
    }


def reference(x, c, ctx, c_ctx, ada_w, ada_b, norm_mix, norm_ffn, ret_w_qkvg, ret_w_o,
              conv_w_in, conv_w, conv_w_out, ffn_w1, ffn_w3, ffn_w2, final_norm):
    T = x.shape[1]
    rows = T // GRID_W
    row = jnp.repeat(jnp.arange(rows, dtype=jnp.int32), GRID_W)
    col = jnp.tile(jnp.arange(GRID_W, dtype=jnp.int32), rows)

    hx, hc = x, ctx
    for i in range(DEPTH):
        mixer = i % N_MIXERS
        ctx_needed = any(j % N_MIXERS == 0 for j in range(i + 1, DEPTH))
        sh1, sc1, g1, sh2, sc2, g2 = [t[:, None, :] for t in ada_params(c, ada_w[i], ada_b[i])]
        ax = modulate(rms_norm(hx, norm_mix[i]), sh1, sc1)
        if mixer == 0 or ctx_needed:
            csh1, csc1, cg1, csh2, csc2, cg2 = ada_params(c_ctx, ada_w[i], ada_b[i])
            ac = modulate(rms_norm(hc, norm_mix[i]), csh1, csc1)

        if mixer == 0:
            r = i // N_MIXERS
            yx, yc = retention_mixer(ax, ac, ret_w_qkvg[r], ret_w_o[r], row, col, ctx_needed)
        else:
            r = i // N_MIXERS
            yx = conv_mixer(ax, conv_w_in[r], conv_w[r], conv_w_out[r])
            yc = conv_mixer(ac, conv_w_in[r], conv_w[r], conv_w_out[r]) if ctx_needed else None

        hx = hx + g1 * yx
        fx = modulate(rms_norm(hx, norm_ffn[i]), sh2, sc2)
        hx = hx + g2 * swiglu(fx, ffn_w1[i], ffn_w3[i], ffn_w2[i])

        if ctx_needed:
            hc = hc + cg1 * yc
            fc = modulate(rms_norm(hc, norm_ffn[i]), csh2, csc2)
            hc = hc + cg2 * swiglu(fc, ffn_w1[i], ffn_w3[i], ffn_w2[i])

    return rms_norm(hx, final_norm)
```

```python
import functools

import jax
import jax.numpy as jnp
import numpy as np
from jax import lax
from jax.experimental import pallas as pl
from jax.experimental.pallas import tpu as pltpu

D_MODEL = 1024
GRID_W = 64
RET_HEADS = 4
RET_QK_DIM = D_MODEL // RET_HEADS
RET_V_DIM = 2 * D_MODEL // RET_HEADS
DECAY_EXP_FWD = 5.0
DECAY_EXP_BWD = 5.5
ROPE_BASE = 10000.0
EPS = 1e-6

RET_CHUNK = 256
LANES = 128
HALO = 8
VMEM_LIMIT = 56 * 1024 * 1024

BF16 = jnp.bfloat16
F32 = jnp.float32


def _const_spec(shape):
    return pl.BlockSpec(shape, lambda *_: (0,) * len(shape), pipeline_mode=pl.Buffered(1))


def _params(*sem):
    return pltpu.CompilerParams(dimension_semantics=sem, vmem_limit_bytes=VMEM_LIMIT)


def _norm_mod(x, gain, shift, scale):
    y = x * lax.rsqrt(jnp.mean(x * x, axis=-1, keepdims=True) + EPS) * gain
    return y * (1.0 + scale) + shift


def _dot(a, b):
    return jnp.dot(a, b, preferred_element_type=F32)


def _dot_nt(a, b):
    return lax.dot_general(a, b, (((1,), (1,)), ((), ())), preferred_element_type=F32)


def _dot_tn(a, b):
    return lax.dot_general(a, b, (((0,), (0,)), ((), ())), preferred_element_type=F32)


def _ada_kernel(cond_ref, w_ref, b_ref, o_ref):
    c = cond_ref[...]
    s = c * jax.nn.sigmoid(c)
    hi = s.astype(BF16)
    lo = (s - hi.astype(F32)).astype(BF16)
    w = w_ref[0].astype(BF16)
    o_ref[0] = _dot(hi, w) + _dot(lo, w) + b_ref[0]


def _ada(cond, ada_w, ada_b, bn=1536):
    depth, d, n = ada_w.shape
    return pl.pallas_call(
        _ada_kernel,
        grid=(depth, n // bn),
        in_specs=[
            pl.BlockSpec((HALO, d), lambda l, j: (0, 0)),
            pl.BlockSpec((1, d, bn), lambda l, j: (l, 0, j)),
            pl.BlockSpec((1, 1, bn), lambda l, j: (l, 0, j)),
        ],
        out_specs=pl.BlockSpec((1, HALO, bn), lambda l, j: (l, 0, j)),
        out_shape=jax.ShapeDtypeStruct((depth, HALO, n), F32),
        compiler_params=_params("parallel", "parallel"),
        name="ada",
    )(cond, ada_w, ada_b.reshape(depth, 1, n))


def _mod_rows(mod_ref, row, first):
    d = D_MODEL
    return [mod_ref[pl.ds(row, 1), (first + j) * d:(first + j + 1) * d] for j in range(3)]


def _rope(r, cos, sin):
    outs = []
    for blk in range(r.shape[1] // LANES):
        xb = r[:, blk * LANES:(blk + 1) * LANES]
        tb = (blk % 2) * LANES
        outs.append(xb * cos[:, tb:tb + LANES]
                    + pltpu.roll(xb, LANES // 2, axis=1) * sin[:, tb:tb + LANES])
    return jnp.concatenate(outs, axis=1)


def _qkvg_kernel(x_ref, mod_ref, gain_ref, w_ref, cos_ref, sin_ref, q_ref, k_ref, v_ref, g_ref):
    d = D_MODEL
    shift, scale, _ = _mod_rows(mod_ref, pl.program_id(0), 0)
    ax = _norm_mod(x_ref[0], gain_ref[...], shift, scale).astype(BF16)
    cos, sin = cos_ref[...], sin_ref[...]
    q_ref[0] = _rope(_dot(ax, w_ref[:, 0:d]), cos, sin).astype(BF16)
    k_ref[0] = (_rope(_dot(ax, w_ref[:, d:2 * d]), cos, sin) * RET_QK_DIM ** -0.5).astype(BF16)
    for j in range(2):
        v_ref[0, :, j * d:(j + 1) * d] = _dot(ax, w_ref[:, (2 + j) * d:(3 + j) * d]).astype(BF16)
        g_ref[0, :, j * d:(j + 1) * d] = _dot(ax, w_ref[:, (4 + j) * d:(5 + j) * d]).astype(BF16)


def _qkvg(x, mod, gain, w, cos, sin, tm=512):
    bsz, t, d = x.shape
    row = lambda b, i: (b, i, 0)
    return pl.pallas_call(
        _qkvg_kernel,
        grid=(bsz, t // tm),
        in_specs=[
            pl.BlockSpec((1, tm, d), row),
            _const_spec(mod.shape),
            _const_spec(gain.shape),
            _const_spec(w.shape),
            pl.BlockSpec((tm, 2 * LANES), lambda b, i: (i, 0)),
            pl.BlockSpec((tm, 2 * LANES), lambda b, i: (i, 0)),
        ],
        out_specs=[
            pl.BlockSpec((1, tm, d), row),
            pl.BlockSpec((1, tm, d), row),
            pl.BlockSpec((1, tm, 2 * d), row),
            pl.BlockSpec((1, tm, 2 * d), row),
        ],
        out_shape=[
            jax.ShapeDtypeStruct((bsz, t, d), BF16),
            jax.ShapeDtypeStruct((bsz, t, d), BF16),
            jax.ShapeDtypeStruct((bsz, t, 2 * d), BF16),
            jax.ShapeDtypeStruct((bsz, t, 2 * d), BF16),
        ],
        compiler_params=_params("parallel", "parallel"),
        name="qkvg",
    )(x, mod, gain, w, cos, sin)


def _ctx_kv_kernel(x_ref, mod_ref, gain_ref, w_ref, k_ref, v_ref):
    d = D_MODEL
    shift, scale, _ = _mod_rows(mod_ref, 2, 0)
    ax = _norm_mod(x_ref[0], gain_ref[...], shift, scale).astype(BF16)
    k_ref[0] = (_dot(ax, w_ref[:, 0:d]) * RET_QK_DIM ** -0.5).astype(BF16)
    for j in range(2):
        v_ref[0, :, j * d:(j + 1) * d] = _dot(ax, w_ref[:, (1 + j) * d:(2 + j) * d]).astype(BF16)


def _ctx_kv(ctx, mod, gain, w_kv):
    bsz, t, d = ctx.shape
    row = lambda b: (b, 0, 0)
    return pl.pallas_call(
        _ctx_kv_kernel,
        grid=(bsz,),
        in_specs=[
            pl.BlockSpec((1, t, d), row),
            _const_spec(mod.shape),
            _const_spec(gain.shape),
            _const_spec(w_kv.shape),
        ],
        out_specs=[pl.BlockSpec((1, t, d), row), pl.BlockSpec((1, t, 2 * d), row)],
        out_shape=[
            jax.ShapeDtypeStruct((bsz, t, d), BF16),
            jax.ShapeDtypeStruct((bsz, t, 2 * d), BF16),
        ],
        compiler_params=_params("parallel"),
        name="ctx_kv",
    )(ctx, mod, gain, w_kv)


def _decay_tables(ctx_len):
    c = RET_CHUNK
    h = np.arange(RET_HEADS, dtype=np.float64)
    lg_f = np.log1p(-np.exp2(-DECAY_EXP_FWD - h))[:, None]
    lg_b = np.log1p(-np.exp2(-DECAY_EXP_BWD - h))[:, None]
    idx = np.arange(c, dtype=np.float64)
    diff = idx[:, None] - idx[None, :]
    mask = (np.where(diff >= 0, np.exp(np.maximum(diff, 0.0)[None] * lg_f[:, :, None]), 0.0)
            + np.where(diff <= 0, np.exp(np.maximum(-diff, 0.0)[None] * lg_b[:, :, None]), 0.0))
    lane = lambda a: jnp.asarray(np.repeat(a[:, :, None], LANES, axis=2), F32)
    cidx = np.arange(ctx_len, dtype=np.float64)
    return dict(
        mask=jnp.asarray(mask, F32),
        qdec_f=lane(np.exp((idx[None] + 1.0) * lg_f)),
        kdec_f=lane(np.exp((c - 1.0 - idx[None]) * lg_f)),
        qdec_b=lane(np.exp((c - idx[None]) * lg_b)),
        kdec_b=lane(np.exp(idx[None] * lg_b)),
        ctxw_f=lane(np.exp((ctx_len - 1.0 - cidx[None]) * lg_f)),
        ctxw_b=lane(np.exp(cidx[None] * lg_b)),
        cdec_f=[float(v) for v in np.exp(c * lg_f[:, 0])],
        cdec_b=[float(v) for v in np.exp(c * lg_b[:, 0])],
    )


def _lanes(tab, n):
    return jnp.concatenate([tab] * (n // LANES), axis=1)


def _head(ref, rows, h, width):
    return ref[0, rows, h * width:(h + 1) * width]


def _scaled_bf16(k, tab):
    return (k.astype(F32) * _lanes(tab, k.shape[1])).astype(BF16)


def _init_state(s_ref, kc_ref, vc_ref, ctxw_ref):
    every = slice(None)
    for h in range(RET_HEADS):
        kw = _scaled_bf16(_head(kc_ref, every, h, RET_QK_DIM), ctxw_ref[h])
        s_ref[h] = _dot_tn(kw, _head(vc_ref, every, h, RET_V_DIM))


def _ret_fwd_kernel(q_ref, k_ref, v_ref, kc_ref, vc_ref, mask_ref, qdec_ref, kdec_ref, ctxw_ref,
                    o_ref, s_ref, *, cdec, n_chunks):
    @pl.when(pl.program_id(1) == 0)
    def _():
        _init_state(s_ref, kc_ref, vc_ref, ctxw_ref)

    def chunk(c, carry):
        rows = pl.ds(pl.multiple_of(c * RET_CHUNK, RET_CHUNK), RET_CHUNK)
        for h in range(RET_HEADS):
            q = _head(q_ref, rows, h, RET_QK_DIM)
            k = _head(k_ref, rows, h, RET_QK_DIM)
            v = _head(v_ref, rows, h, RET_V_DIM)
            p = (_dot_nt(q, k) * mask_ref[h]).astype(BF16)
            s = s_ref[h]
            cross = _dot(q, s.astype(BF16)) * _lanes(qdec_ref[h], RET_V_DIM)
            o_ref[0, rows, h * RET_V_DIM:(h + 1) * RET_V_DIM] = _dot(p, v) + cross
            s_ref[h] = s * cdec[h] + _dot_tn(_scaled_bf16(k, kdec_ref[h]), v)
        return carry

    lax.fori_loop(0, n_chunks, chunk, 0)


def _ret_bwd_kernel(q_ref, k_ref, v_ref, kc_ref, vc_ref, qdec_ref, kdec_ref, ctxw_ref,
                    op_ref, g_ref, x_ref, mod_ref, wo_ref, o_ref, s_ref, y_ref, *, cdec, n_chunks):
    @pl.when(pl.program_id(1) == 0)
    def _():
        _init_state(s_ref, kc_ref, vc_ref, ctxw_ref)

    def chunk(i, carry):
        c = n_chunks - 1 - i
        rows = pl.ds(pl.multiple_of(c * RET_CHUNK, RET_CHUNK), RET_CHUNK)
        for h in range(RET_HEADS):
            q = _head(q_ref, rows, h, RET_QK_DIM)
            k = _head(k_ref, rows, h, RET_QK_DIM)
            v = _head(v_ref, rows, h, RET_V_DIM)
            s = s_ref[h]
            o = _head(op_ref, rows, h, RET_V_DIM) + _dot(q, s.astype(BF16)) * _lanes(qdec_ref[h], RET_V_DIM)
            o = o * lax.rsqrt(jnp.mean(o * o, axis=-1, keepdims=True) + EPS)
            g = _head(g_ref, rows, h, RET_V_DIM).astype(F32)
            y_ref[rows, h * RET_V_DIM:(h + 1) * RET_V_DIM] = (g * jax.nn.sigmoid(g) * o).astype(BF16)
            s_ref[h] = s * cdec[h] + _dot_tn(_scaled_bf16(k, kdec_ref[h]), v)
        return carry

    lax.fori_loop(0, n_chunks, chunk, 0)
    gate = _mod_rows(mod_ref, pl.program_id(0), 0)[2]
    o_ref[0] = x_ref[0] + gate * _dot(y_ref[...], wo_ref[...])


def _retention(x, q, k, v, g, kc, vc, mod, w_o, blk_fwd=1024, blk_bwd=512):
    bsz, t, d = x.shape
    ctx_len = kc.shape[1]
    tabs = _decay_tables(ctx_len)
    per_b = lambda b, i: (b, 0, 0)
    state = pltpu.VMEM((RET_HEADS, RET_QK_DIM, RET_V_DIM), F32)

    blk = blk_fwd
    n_blk, n_chunks = t // blk, blk // RET_CHUNK
    fwd = lambda b, i: (b, i, 0)
    o_part = pl.pallas_call(
        functools.partial(_ret_fwd_kernel, cdec=tabs["cdec_f"], n_chunks=n_chunks),
        grid=(bsz, n_blk),
        in_specs=[
            pl.BlockSpec((1, blk, d), fwd),
            pl.BlockSpec((1, blk, d), fwd),
            pl.BlockSpec((1, blk, 2 * d), fwd),
            pl.BlockSpec((1, ctx_len, d), per_b),
            pl.BlockSpec((1, ctx_len, 2 * d), per_b),
            _const_spec(tabs["mask"].shape),
            _const_spec(tabs["qdec_f"].shape),
            _const_spec(tabs["kdec_f"].shape),
            _const_spec(tabs["ctxw_f"].shape),
        ],
        out_specs=pl.BlockSpec((1, blk, 2 * d), fwd),
        out_shape=jax.ShapeDtypeStruct((bsz, t, 2 * d), F32),
        scratch_shapes=[state],
        compiler_params=_params("parallel", "arbitrary"),
        name="ret_fwd",
    )(q, k, v, kc, vc, tabs["mask"], tabs["qdec_f"], tabs["kdec_f"], tabs["ctxw_f"])

    blk = blk_bwd
    n_blk, n_chunks = t // blk, blk // RET_CHUNK
    bwd = lambda b, i: (b, n_blk - 1 - i, 0)
    return pl.pallas_call(
        functools.partial(_ret_bwd_kernel, cdec=tabs["cdec_b"], n_chunks=n_chunks),
        grid=(bsz, n_blk),
        in_specs=[
            pl.BlockSpec((1, blk, d), bwd),
            pl.BlockSpec((1, blk, d), bwd),
            pl.BlockSpec((1, blk, 2 * d), bwd),
            pl.BlockSpec((1, ctx_len, d), per_b),
            pl.BlockSpec((1, ctx_len, 2 * d), per_b),
            _const_spec(tabs["qdec_b"].shape),
            _const_spec(tabs["kdec_b"].shape),
            _const_spec(tabs["ctxw_b"].shape),
            pl.BlockSpec((1, blk, 2 * d), bwd),
            pl.BlockSpec((1, blk, 2 * d), bwd),
            pl.BlockSpec((1, blk, d), bwd),
            _const_spec(mod.shape),
            _const_spec(w_o.shape),
        ],
        out_specs=pl.BlockSpec((1, blk, d), bwd),
        out_shape=jax.ShapeDtypeStruct((bsz, t, d), F32),
        scratch_shapes=[state, pltpu.VMEM((blk, 2 * d), BF16)],
        compiler_params=_params("parallel", "arbitrary"),
        name="ret_bwd",
    )(q, k, v, kc, vc, tabs["qdec_b"], tabs["kdec_b"], tabs["ctxw_b"], o_part, g, x, mod, w_o)


def _ffn_chunks(f):
    step = 768
    return [(c, min(c + step, f)) for c in range(0, f, step)]


def _ffn_kernel(x_ref, mod_ref, gain_ref, w1_ref, w3_ref, w2_ref, fgain_ref, o_ref, *, final_norm):
    shift, scale, gate = _mod_rows(mod_ref, pl.program_id(0), 3)
    x = x_ref[0]
    fx = _norm_mod(x, gain_ref[...], shift, scale).astype(BF16)
    acc = None
    for c0, c1 in _ffn_chunks(w1_ref.shape[1]):
        a = _dot(fx, w1_ref[:, c0:c1])
        hid = (a * jax.nn.sigmoid(a) * _dot(fx, w3_ref[:, c0:c1])).astype(BF16)
        part = _dot(hid, w2_ref[c0:c1, :])
        acc = part if acc is None else acc + part
    out = x + gate * acc
    if final_norm:
        out = out * lax.rsqrt(jnp.mean(out * out, axis=-1, keepdims=True) + EPS) * fgain_ref[...]
    o_ref[0] = out


def _ffn(x, mod, gain, w1, w3, w2, fgain, final_norm, tm=512):
    bsz, t, d = x.shape
    row = lambda b, i: (b, i, 0)
    return pl.pallas_call(
        functools.partial(_ffn_kernel, final_norm=final_norm),
        grid=(bsz, t // tm),
        in_specs=[
            pl.BlockSpec((1, tm, d), row),
            _const_spec(mod.shape),
            _const_spec(gain.shape),
            _const_spec(w1.shape),
            _const_spec(w3.shape),
            _const_spec(w2.shape),
            _const_spec(fgain.shape),
        ],
        out_specs=pl.BlockSpec((1, tm, d), row),
        out_shape=jax.ShapeDtypeStruct((bsz, t, d), F32),
        compiler_params=_params("parallel", "parallel"),
        name="ffn_final" if final_norm else "ffn",
    )(x, mod, gain, w1, w3, w2, fgain)


def _conv_kernel(x_ref, prev_ref, next_ref, mod_ref, gain_ref, win_ref, cw_ref, wout_ref, o_ref, u_ref):
    d = D_MODEL
    tm = x_ref.shape[1]
    i, last = pl.program_id(1), pl.num_programs(1) - 1
    shift, scale, gate = _mod_rows(mod_ref, pl.program_id(0), 0)
    x = x_ref[0]
    halo = jnp.concatenate([prev_ref[0], next_ref[0]], axis=0)
    ax = jnp.concatenate([_norm_mod(x, gain_ref[...], shift, scale),
                          _norm_mod(halo, gain_ref[...], shift, scale)], axis=0).astype(BF16)
    u = _dot(ax, win_ref[:, d:2 * d]) * _dot(ax, win_ref[:, 2 * d:3 * d])
    u_ref[0:HALO] = u[tm:tm + HALO] * jnp.where(i == 0, 0.0, 1.0)
    u_ref[HALO:HALO + tm] = u[0:tm]
    u_ref[HALO + tm:] = u[tm + HALO:] * jnp.where(i == last, 0.0, 1.0)
    cw = cw_ref[...]
    conv = (u_ref[pl.ds(HALO - 1, tm)] * cw[0:1] + u[0:tm] * cw[1:2]
            + u_ref[pl.ds(HALO + 1, tm)] * cw[2:3])
    b_gate = _dot(ax[0:tm], win_ref[:, 0:d])
    o_ref[0] = x + gate * _dot((b_gate * conv).astype(BF16), wout_ref[...])


def _conv(x, mod, gain, w_in, conv_w, w_out, tm=512):
    bsz, t, d = x.shape
    per = tm // HALO
    n_halo = t // HALO
    row = lambda b, i: (b, i, 0)
    return pl.pallas_call(
        _conv_kernel,
        grid=(bsz, t // tm),
        in_specs=[
            pl.BlockSpec((1, tm, d), row),
            pl.BlockSpec((1, HALO, d), lambda b, i: (b, jnp.maximum(i * per - 1, 0), 0)),
            pl.BlockSpec((1, HALO, d), lambda b, i: (b, jnp.minimum((i + 1) * per, n_halo - 1), 0)),
            _const_spec(mod.shape),
            _const_spec(gain.shape),
            _const_spec(w_in.shape),
            _const_spec(conv_w.shape),
            _const_spec(w_out.shape),
        ],
        out_specs=pl.BlockSpec((1, tm, d), row),
        out_shape=jax.ShapeDtypeStruct((bsz, t, d), F32),
        scratch_shapes=[pltpu.VMEM((tm + 2 * HALO, d), F32)],
        compiler_params=_params("parallel", "parallel"),
        name="conv",
    )(x, x, x, mod, gain, w_in, conv_w, w_out)


def _rope_tables(t):
    n = LANES // 2
    freqs = ROPE_BASE ** (-np.arange(n, dtype=np.float64) / n)
    def tabs(pos):
        ang = pos[:, None] * freqs[None, :]
        return (np.concatenate([np.cos(ang), np.cos(ang)], axis=1),
                np.concatenate([-np.sin(ang), np.sin(ang)], axis=1))
    rows = t // GRID_W
    rcos, rsin = tabs(np.arange(rows, dtype=np.float64))
    ccos, csin = tabs(np.arange(GRID_W, dtype=np.float64))
    full = lambda r, c: jnp.concatenate(
        [jnp.repeat(jnp.asarray(r, F32), GRID_W, axis=0), jnp.tile(jnp.asarray(c, F32), (rows, 1))], axis=1)
    return full(rcos, ccos), full(rsin, csin)


def kernel(x, c, ctx, c_ctx, ada_w, ada_b, norm_mix, norm_ffn, ret_w_qkvg, ret_w_o, conv_w_in, conv_w,
           conv_w_out, ffn_w1, ffn_w3, ffn_w2, final_norm):
    bsz, t, d = x.shape
    cond = jnp.concatenate([c, c_ctx[None], jnp.zeros((HALO - bsz - 1, d), F32)], axis=0)
    mod = _ada(cond, ada_w, ada_b)
    row2 = lambda a: a.reshape(1, d)
    fgain = row2(final_norm)
    cos, sin = _rope_tables(t)

    w_qkvg = ret_w_qkvg[0].astype(BF16)
    q, k, v, g = _qkvg(x, mod[0], row2(norm_mix[0]), w_qkvg, cos, sin)
    kc, vc = _ctx_kv(ctx, mod[0], row2(norm_mix[0]), w_qkvg[:, d:4 * d])
    hx = _retention(x, q, k, v, g, kc, vc, mod[0], ret_w_o[0].astype(BF16))
    hx = _ffn(hx, mod[0], row2(norm_ffn[0]), ffn_w1[0].astype(BF16), ffn_w3[0].astype(BF16),
              ffn_w2[0].astype(BF16), fgain, final_norm=False)

    hx = _conv(hx, mod[1], row2(norm_mix[1]), conv_w_in[0].astype(BF16), conv_w[0],
               conv_w_out[0].astype(BF16))
    return _ffn(hx, mod[1], row2(norm_ffn[1]), ffn_w1[1].astype(BF16), ffn_w3[1].astype(BF16),
                ffn_w2[1].astype(BF16), fgain, final_norm=True)
```

```python
import functools

import jax
import jax.numpy as jnp
import numpy as np
from jax import lax
from jax.experimental import pallas as pl
from jax.experimental.pallas import tpu as pltpu

D_MODEL = 1024
GRID_W = 64
RET_HEADS = 4
RET_QK_DIM = D_MODEL // RET_HEADS
RET_V_DIM = 2 * D_MODEL // RET_HEADS
DECAY_EXP_FWD = 5.0
DECAY_EXP_BWD = 5.5
ROPE_BASE = 10000.0
EPS = 1e-6

RET_CHUNK = 256
LANES = 128
HALO = 8
VMEM_LIMIT = 56 * 1024 * 1024

BF16 = jnp.bfloat16
F32 = jnp.float32


def _const_spec(shape):
    return pl.BlockSpec(shape, lambda *_: (0,) * len(shape), pipeline_mode=pl.Buffered(1))


def _params(*sem):
    return pltpu.CompilerParams(dimension_semantics=sem, vmem_limit_bytes=VMEM_LIMIT)


def _norm_mod(x, gain, shift, scale):
    y = x * lax.rsqrt(jnp.mean(x * x, axis=-1, keepdims=True) + EPS) * gain
    return y * (1.0 + scale) + shift


def _dot(a, b):
    return jnp.dot(a, b, preferred_element_type=F32)


def _dot_nt(a, b):
    return lax.dot_general(a, b, (((1,), (1,)), ((), ())), preferred_element_type=F32)


def _dot_tn(a, b):
    return lax.dot_general(a, b, (((0,), (0,)), ((), ())), preferred_element_type=F32)


def _ada_kernel(cond_ref, w_ref, b_ref, o_ref):
    c = cond_ref[...]
    s = c * jax.nn.sigmoid(c)
    hi = s.astype(BF16)
    lo = (s - hi.astype(F32)).astype(BF16)
    w = w_ref[0].astype(BF16)
    o_ref[0] = _dot(hi, w) + _dot(lo, w) + b_ref[0]


def _ada(cond, ada_w, ada_b, bn=1536):
    depth, d, n = ada_w.shape
    return pl.pallas_call(
        _ada_kernel,
        grid=(depth, n // bn),
        in_specs=[
            pl.BlockSpec((HALO, d), lambda l, j: (0, 0)),
            pl.BlockSpec((1, d, bn), lambda l, j: (l, 0, j)),
            pl.BlockSpec((1, 1, bn), lambda l, j: (l, 0, j)),
        ],
        out_specs=pl.BlockSpec((1, HALO, bn), lambda l, j: (l, 0, j)),
        out_shape=jax.ShapeDtypeStruct((depth, HALO, n), F32),
        compiler_params=_params("parallel", "parallel"),
        name="ada",
    )(cond, ada_w, ada_b.reshape(depth, 1, n))


def _mod_rows(mod_ref, row, first):
    d = D_MODEL
    return [mod_ref[pl.ds(row, 1), (first + j) * d:(first + j + 1) * d] for j in range(3)]


def _decay_tables(ctx_len):
    c = RET_CHUNK
    h = np.arange(RET_HEADS, dtype=np.float64)
    lg_f = np.log1p(-np.exp2(-DECAY_EXP_FWD - h))[:, None]
    lg_b = np.log1p(-np.exp2(-DECAY_EXP_BWD - h))[:, None]
    idx = np.arange(c, dtype=np.float64)
    diff = idx[:, None] - idx[None, :]
    mask = (np.where(diff >= 0, np.exp(np.maximum(diff, 0.0)[None] * lg_f[:, :, None]), 0.0)
            + np.where(diff <= 0, np.exp(np.maximum(-diff, 0.0)[None] * lg_b[:, :, None]), 0.0))
    lane = lambda a: jnp.asarray(np.repeat(a[:, :, None], LANES, axis=2), F32)
    cidx = np.arange(ctx_len, dtype=np.float64)
    return dict(
        mask=jnp.asarray(mask, F32),
        qdec_f=lane(np.exp((idx[None] + 1.0) * lg_f)),
        kdec_f=lane(np.exp((c - 1.0 - idx[None]) * lg_f)),
        qdec_b=lane(np.exp((c - idx[None]) * lg_b)),
        kdec_b=lane(np.exp(idx[None] * lg_b)),
        ctxw_f=lane(np.exp((ctx_len - 1.0 - cidx[None]) * lg_f)),
        ctxw_b=lane(np.exp(cidx[None] * lg_b)),
        cdec_f=[float(v) for v in np.exp(c * lg_f[:, 0])],
        cdec_b=[float(v) for v in np.exp(c * lg_b[:, 0])],
    )


def _lanes(tab, n):
    return jnp.concatenate([tab] * (n // LANES), axis=1)


def _head(ref, rows, h, width):
    return ref[0, rows, h * width:(h + 1) * width]


def _scaled_bf16(k, tab):
    return (k.astype(F32) * _lanes(tab, k.shape[1])).astype(BF16)


def _init_state(s_ref, kc_ref, vc_ref, ctxw_ref):
    every = slice(None)
    for h in range(RET_HEADS):
        kw = _scaled_bf16(_head(kc_ref, every, h, RET_QK_DIM), ctxw_ref[h])
        s_ref[h] = _dot_tn(kw, _head(vc_ref, every, h, RET_V_DIM))


def _ctx_kv_kernel(x_ref, mod_ref, gain_ref, wk_ref, wv_ref, k_ref, v_ref):
    d = D_MODEL
    shift, scale, _ = _mod_rows(mod_ref, 2, 0)
    ax = _norm_mod(x_ref[0], gain_ref[...], shift, scale).astype(BF16)
    k_ref[0] = (_dot(ax, wk_ref[...]) * RET_QK_DIM ** -0.5).astype(BF16)
    for j in range(2):
        v_ref[0, :, j * d:(j + 1) * d] = _dot(ax, wv_ref[:, j * d:(j + 1) * d]).astype(BF16)


def _ctx_kv(ctx, mod, gain, w_qkvg):
    bsz, t, d = ctx.shape
    row = lambda b: (b, 0, 0)
    return pl.pallas_call(
        _ctx_kv_kernel,
        grid=(bsz,),
        in_specs=[
            pl.BlockSpec((1, t, d), row),
            _const_spec(mod.shape),
            _const_spec(gain.shape),
            pl.BlockSpec((d, d), lambda b: (0, 1), pipeline_mode=pl.Buffered(1)),
            pl.BlockSpec((d, 2 * d), lambda b: (0, 1), pipeline_mode=pl.Buffered(1)),
        ],
        out_specs=[pl.BlockSpec((1, t, d), row), pl.BlockSpec((1, t, 2 * d), row)],
        out_shape=[
            jax.ShapeDtypeStruct((bsz, t, d), BF16),
            jax.ShapeDtypeStruct((bsz, t, 2 * d), BF16),
        ],
        compiler_params=_params("parallel"),
        name="ctx_kv",
    )(ctx, mod, gain, w_qkvg, w_qkvg)


def _rope_tile(rcos_ref, rsin_ref, ccos_ref, csin_ref, tile, tm):
    n_rows = tm // GRID_W
    row0 = tile * n_rows
    def build(r_ref, c_ref):
        rpart = jnp.concatenate(
            [jnp.broadcast_to(r_ref[pl.ds(row0 + r, 1), :], (GRID_W, LANES)) for r in range(n_rows)], axis=0)
        cpart = jnp.concatenate([c_ref[...]] * n_rows, axis=0)
        return jnp.concatenate([rpart, cpart], axis=1)
    return build(rcos_ref, ccos_ref), build(rsin_ref, csin_ref)


def _rope(r, cos, sin):
    outs = []
    for blk in range(r.shape[1] // LANES):
        xb = r[:, blk * LANES:(blk + 1) * LANES]
        tb = (blk % 2) * LANES
        outs.append(xb * cos[:, tb:tb + LANES]
                    + pltpu.roll(xb, LANES // 2, axis=1) * sin[:, tb:tb + LANES])
    return jnp.concatenate(outs, axis=1)


def _qkvg_kernel(x_ref, mod_ref, gain_ref, w_ref, rcos_ref, rsin_ref, ccos_ref, csin_ref,
                 kdec_ref, ctxw_ref, kc_ref, vc_ref, q_ref, k_ref, v_ref, g_ref, sb_ref, s_ref, *, cdec):
    d = D_MODEL
    tm = x_ref.shape[1]
    step = pl.program_id(1)
    tile = pl.num_programs(1) - 1 - step

    @pl.when(step == 0)
    def _():
        _init_state(s_ref, kc_ref, vc_ref, ctxw_ref)

    shift, scale, _ = _mod_rows(mod_ref, pl.program_id(0), 0)
    ax = _norm_mod(x_ref[0], gain_ref[...], shift, scale).astype(BF16)
    cos, sin = _rope_tile(rcos_ref, rsin_ref, ccos_ref, csin_ref, tile, tm)
    q_ref[0] = _rope(_dot(ax, w_ref[:, 0:d]), cos, sin).astype(BF16)
    kf = _rope(_dot(ax, w_ref[:, d:2 * d]), cos, sin) * RET_QK_DIM ** -0.5
    k_ref[0] = kf.astype(BF16)
    for j in range(2):
        v_ref[0, :, j * d:(j + 1) * d] = _dot(ax, w_ref[:, (2 + j) * d:(3 + j) * d]).astype(BF16)
        g_ref[0, :, j * d:(j + 1) * d] = _dot(ax, w_ref[:, (4 + j) * d:(5 + j) * d]).astype(BF16)

    for c in reversed(range(tm // RET_CHUNK)):
        rows = slice(c * RET_CHUNK, (c + 1) * RET_CHUNK)
        for h in range(RET_HEADS):
            s = s_ref[h]
            sb_ref[0, c, h] = s.astype(BF16)
            kd = (kf[rows, h * RET_QK_DIM:(h + 1) * RET_QK_DIM] * _lanes(kdec_ref[h], RET_QK_DIM)).astype(BF16)
            s_ref[h] = s * cdec[h] + _dot_tn(kd, _head(v_ref, rows, h, RET_V_DIM))


def _qkvg(x, mod, gain, w, kc, vc, tabs, rope, tm=512):
    bsz, t, d = x.shape
    n_t = t // tm
    row = lambda b, i: (b, n_t - 1 - i, 0)
    per_b = lambda b, i: (b, 0, 0)
    n_c = tm // RET_CHUNK
    sb_block = (1, n_c, RET_HEADS, RET_QK_DIM, RET_V_DIM)
    return pl.pallas_call(
        functools.partial(_qkvg_kernel, cdec=tabs["cdec_b"]),
        grid=(bsz, n_t),
        in_specs=[
            pl.BlockSpec((1, tm, d), row),
            _const_spec(mod.shape),
            _const_spec(gain.shape),
            _const_spec(w.shape),
            *[_const_spec(r.shape) for r in rope],
            _const_spec(tabs["kdec_b"].shape),
            _const_spec(tabs["ctxw_b"].shape),
            pl.BlockSpec((1,) + kc.shape[1:], per_b),
            pl.BlockSpec((1,) + vc.shape[1:], per_b),
        ],
        out_specs=[
            pl.BlockSpec((1, tm, d), row),
            pl.BlockSpec((1, tm, d), row),
            pl.BlockSpec((1, tm, 2 * d), row),
            pl.BlockSpec((1, tm, 2 * d), row),
            pl.BlockSpec(sb_block, lambda b, i: (b, n_t - 1 - i, 0, 0, 0)),
        ],
        out_shape=[
            jax.ShapeDtypeStruct((bsz, t, d), BF16),
            jax.ShapeDtypeStruct((bsz, t, d), BF16),
            jax.ShapeDtypeStruct((bsz, t, 2 * d), BF16),
            jax.ShapeDtypeStruct((bsz, t, 2 * d), BF16),
            jax.ShapeDtypeStruct((bsz, t // RET_CHUNK) + sb_block[2:], BF16),
        ],
        scratch_shapes=[pltpu.VMEM((RET_HEADS, RET_QK_DIM, RET_V_DIM), F32)],
        compiler_params=_params("parallel", "arbitrary"),
        name="qkvg",
    )(x, mod, gain, w, *rope, tabs["kdec_b"], tabs["ctxw_b"], kc, vc)


def _ret_kernel(q_ref, k_ref, v_ref, g_ref, sb_ref, x_ref, kc_ref, vc_ref, mask_ref, qdf_ref, kdf_ref,
                qdb_ref, ctxw_ref, mod_ref, wo_ref, o_ref, s_ref, y_ref, *, cdec):
    @pl.when(pl.program_id(1) == 0)
    def _():
        _init_state(s_ref, kc_ref, vc_ref, ctxw_ref)

    for c in range(x_ref.shape[1] // RET_CHUNK):
        rows = slice(c * RET_CHUNK, (c + 1) * RET_CHUNK)
        for h in range(RET_HEADS):
            q = _head(q_ref, rows, h, RET_QK_DIM)
            k = _head(k_ref, rows, h, RET_QK_DIM)
            v = _head(v_ref, rows, h, RET_V_DIM)
            p = (_dot_nt(q, k) * mask_ref[h]).astype(BF16)
            s = s_ref[h]
            o = (_dot(p, v)
                 + _dot(q, s.astype(BF16)) * _lanes(qdf_ref[h], RET_V_DIM)
                 + _dot(q, sb_ref[0, c, h]) * _lanes(qdb_ref[h], RET_V_DIM))
            o = o * lax.rsqrt(jnp.mean(o * o, axis=-1, keepdims=True) + EPS)
            g = _head(g_ref, rows, h, RET_V_DIM).astype(F32)
            y_ref[rows, h * RET_V_DIM:(h + 1) * RET_V_DIM] = (g * jax.nn.sigmoid(g) * o).astype(BF16)
            s_ref[h] = s * cdec[h] + _dot_tn(_scaled_bf16(k, kdf_ref[h]), v)

    gate = _mod_rows(mod_ref, pl.program_id(0), 0)[2]
    o_ref[0] = x_ref[0] + gate * _dot(y_ref[...], wo_ref[...])


def _retention(x, q, k, v, g, sb, kc, vc, mod, w_o, tabs, blk=512):
    bsz, t, d = x.shape
    row = lambda b, i: (b, i, 0)
    per_b = lambda b, i: (b, 0, 0)
    sb_block = (1, blk // RET_CHUNK) + sb.shape[2:]
    consts = [tabs["mask"], tabs["qdec_f"], tabs["kdec_f"], tabs["qdec_b"], tabs["ctxw_f"], mod, w_o]
    return pl.pallas_call(
        functools.partial(_ret_kernel, cdec=tabs["cdec_f"]),
        grid=(bsz, t // blk),
        in_specs=[
            pl.BlockSpec((1, blk, d), row),
            pl.BlockSpec((1, blk, d), row),
            pl.BlockSpec((1, blk, 2 * d), row),
            pl.BlockSpec((1, blk, 2 * d), row),
            pl.BlockSpec(sb_block, lambda b, i: (b, i, 0, 0, 0)),
            pl.BlockSpec((1, blk, d), row),
            pl.BlockSpec((1,) + kc.shape[1:], per_b),
            pl.BlockSpec((1,) + vc.shape[1:], per_b),
            *[_const_spec(a.shape) for a in consts],
        ],
        out_specs=pl.BlockSpec((1, blk, d), row),
        out_shape=jax.ShapeDtypeStruct((bsz, t, d), F32),
        scratch_shapes=[pltpu.VMEM((RET_HEADS, RET_QK_DIM, RET_V_DIM), F32), pltpu.VMEM((blk, 2 * d), BF16)],
        compiler_params=_params("parallel", "arbitrary"),
        name="ret",
    )(q, k, v, g, sb, x, kc, vc, *consts)


def _ffn_chunks(f):
    step = 768
    return [(c, min(c + step, f)) for c in range(0, f, step)]


def _ffn_kernel(x_ref, mod_ref, gain_ref, w1_ref, w3_ref, w2_ref, fgain_ref, o_ref, *, final_norm):
    shift, scale, gate = _mod_rows(mod_ref, pl.program_id(0), 3)
    x = x_ref[0]
    fx = _norm_mod(x, gain_ref[...], shift, scale).astype(BF16)
    acc = None
    for c0, c1 in _ffn_chunks(w1_ref.shape[1]):
        a = _dot(fx, w1_ref[:, c0:c1])
        hid = (a * jax.nn.sigmoid(a) * _dot(fx, w3_ref[:, c0:c1])).astype(BF16)
        part = _dot(hid, w2_ref[c0:c1, :])
        acc = part if acc is None else acc + part
    out = x + gate * acc
    if final_norm:
        out = out * lax.rsqrt(jnp.mean(out * out, axis=-1, keepdims=True) + EPS) * fgain_ref[...]
    o_ref[0] = out


def _ffn(x, mod, gain, w1, w3, w2, fgain, final_norm, tm=1024):
    bsz, t, d = x.shape
    row = lambda b, i: (b, i, 0)
    return pl.pallas_call(
        functools.partial(_ffn_kernel, final_norm=final_norm),
        grid=(bsz, t // tm),
        in_specs=[
            pl.BlockSpec((1, tm, d), row),
            _const_spec(mod.shape),
            _const_spec(gain.shape),
            _const_spec(w1.shape),
            _const_spec(w3.shape),
            _const_spec(w2.shape),
            _const_spec(fgain.shape),
        ],
        out_specs=pl.BlockSpec((1, tm, d), row),
        out_shape=jax.ShapeDtypeStruct((bsz, t, d), F32),
        compiler_params=_params("parallel", "parallel"),
        name="ffn_final" if final_norm else "ffn",
    )(x, mod, gain, w1, w3, w2, fgain)


def _conv_kernel(x_ref, prev_ref, next_ref, mod_ref, gain_ref, win_ref, cw_ref, wout_ref, o_ref, u_ref):
    d = D_MODEL
    tm = x_ref.shape[1]
    i, last = pl.program_id(1), pl.num_programs(1) - 1
    shift, scale, gate = _mod_rows(mod_ref, pl.program_id(0), 0)
    x = x_ref[0]
    halo = jnp.concatenate([prev_ref[0], next_ref[0]], axis=0)
    ax = jnp.concatenate([_norm_mod(x, gain_ref[...], shift, scale),
                          _norm_mod(halo, gain_ref[...], shift, scale)], axis=0).astype(BF16)
    u = _dot(ax, win_ref[:, d:2 * d]) * _dot(ax, win_ref[:, 2 * d:3 * d])
    u_ref[0:HALO] = u[tm:tm + HALO] * jnp.where(i == 0, 0.0, 1.0)
    u_ref[HALO:HALO + tm] = u[0:tm]
    u_ref[HALO + tm:] = u[tm + HALO:] * jnp.where(i == last, 0.0, 1.0)
    cw = cw_ref[...]
    conv = (u_ref[pl.ds(HALO - 1, tm)] * cw[0:1] + u[0:tm] * cw[1:2]
            + u_ref[pl.ds(HALO + 1, tm)] * cw[2:3])
    b_gate = _dot(ax[0:tm], win_ref[:, 0:d])
    o_ref[0] = x + gate * _dot((b_gate * conv).astype(BF16), wout_ref[...])


def _conv(x, mod, gain, w_in, conv_w, w_out, tm=512):
    bsz, t, d = x.shape
    per = tm // HALO
    n_halo = t // HALO
    row = lambda b, i: (b, i, 0)
    return pl.pallas_call(
        _conv_kernel,
        grid=(bsz, t // tm),
        in_specs=[
            pl.BlockSpec((1, tm, d), row),
            pl.BlockSpec((1, HALO, d), lambda b, i: (b, jnp.maximum(i * per - 1, 0), 0)),
            pl.BlockSpec((1, HALO, d), lambda b, i: (b, jnp.minimum((i + 1) * per, n_halo - 1), 0)),
            _const_spec(mod.shape),
            _const_spec(gain.shape),
            _const_spec(w_in.shape),
            _const_spec(conv_w.shape),
            _const_spec(w_out.shape),
        ],
        out_specs=pl.BlockSpec((1, tm, d), row),
        out_shape=jax.ShapeDtypeStruct((bsz, t, d), F32),
        scratch_shapes=[pltpu.VMEM((tm + 2 * HALO, d), F32)],
        compiler_params=_params("parallel", "parallel"),
        name="conv",
    )(x, x, x, mod, gain, w_in, conv_w, w_out)


def _rope_tables(t):
    n = LANES // 2
    freqs = ROPE_BASE ** (-np.arange(n, dtype=np.float64) / n)
    def tabs(count):
        ang = np.arange(count, dtype=np.float64)[:, None] * freqs[None, :]
        return (jnp.asarray(np.concatenate([np.cos(ang), np.cos(ang)], axis=1), F32),
                jnp.asarray(np.concatenate([-np.sin(ang), np.sin(ang)], axis=1), F32))
    rcos, rsin = tabs(t // GRID_W)
    ccos, csin = tabs(GRID_W)
    return rcos, rsin, ccos, csin


def kernel(x, c, ctx, c_ctx, ada_w, ada_b, norm_mix, norm_ffn, ret_w_qkvg, ret_w_o, conv_w_in, conv_w,
           conv_w_out, ffn_w1, ffn_w3, ffn_w2, final_norm):
    bsz, t, d = x.shape
    cond = jnp.concatenate([c, c_ctx[None], jnp.zeros((HALO - bsz - 1, d), F32)], axis=0)
    mod = _ada(cond, ada_w, ada_b)
    row2 = lambda a: a.reshape(1, d)
    fgain = row2(final_norm)
    tabs = _decay_tables(ctx.shape[1])

    w_qkvg = ret_w_qkvg[0].astype(BF16)
    kc, vc = _ctx_kv(ctx, mod[0], row2(norm_mix[0]), w_qkvg)
    q, k, v, g, sb = _qkvg(x, mod[0], row2(norm_mix[0]), w_qkvg, kc, vc, tabs, _rope_tables(t))
    hx = _retention(x, q, k, v, g, sb, kc, vc, mod[0], ret_w_o[0].astype(BF16), tabs)
    hx = _ffn(hx, mod[0], row2(norm_ffn[0]), ffn_w1[0].astype(BF16), ffn_w3[0].astype(BF16),
              ffn_w2[0].astype(BF16), fgain, final_norm=False)

    hx = _conv(hx, mod[1], row2(norm_mix[1]), conv_w_in[0].astype(BF16), conv_w[0],
               conv_w_out[0].astype(BF16))
    return _ffn(hx, mod[1], row2(norm_ffn[1]), ffn_w1[1].astype(BF16), ffn_w3[1].astype(BF16),
                ffn_w2[1].astype(BF16), fgain, final_norm=True)
```

```python
import functools

import jax
import jax.numpy as jnp
import numpy as np
from jax import lax
from jax.experimental import pallas as pl
from jax.experimental.pallas import tpu as pltpu

D_MODEL = 1024
GRID_W = 64
RET_HEADS = 4
RET_QK_DIM = D_MODEL // RET_HEADS
RET_V_DIM = 2 * D_MODEL // RET_HEADS
DECAY_EXP_FWD = 5.0
DECAY_EXP_BWD = 5.5
ROPE_BASE = 10000.0
EPS = 1e-6

RET_CHUNK = 256
LANES = 128
HALO = 8
VMEM_LIMIT = 56 * 1024 * 1024

BF16 = jnp.bfloat16
F32 = jnp.float32


def _const_spec(shape):
    return pl.BlockSpec(shape, lambda *_: (0,) * len(shape), pipeline_mode=pl.Buffered(1))


def _params(*sem):
    return pltpu.CompilerParams(dimension_semantics=sem, vmem_limit_bytes=VMEM_LIMIT)


def _norm_mod(x, gain, shift, scale):
    y = x * lax.rsqrt(jnp.mean(x * x, axis=-1, keepdims=True) + EPS) * gain
    return y * (1.0 + scale) + shift


def _dot(a, b):
    return jnp.dot(a, b, preferred_element_type=F32)


def _dot_nt(a, b):
    return lax.dot_general(a, b, (((1,), (1,)), ((), ())), preferred_element_type=F32)


def _dot_tn(a, b):
    return lax.dot_general(a, b, (((0,), (0,)), ((), ())), preferred_element_type=F32)


BF16_ROWS = 16


def _cast_plan(weights, grid):
    n_steps = grid[0] * grid[1]
    in_specs, out_specs, out_shapes, operands = [], [], [], []
    for w, layer in weights:
        _, r, c = w.shape
        n_blk = max(n for n in range(1, n_steps + 1) if r % n == 0 and (r // n) % BF16_ROWS == 0)
        blk = lambda b, i, n_blk=n_blk: jnp.minimum(b * grid[1] + i, n_blk - 1)
        in_specs.append(pl.BlockSpec((1, r // n_blk, c), lambda b, i, blk=blk, layer=layer: (layer, blk(b, i), 0)))
        out_specs.append(pl.BlockSpec((r // n_blk, c), lambda b, i, blk=blk: (blk(b, i), 0)))
        out_shapes.append(jax.ShapeDtypeStruct((r, c), BF16))
        operands.append(w)
    return in_specs, out_specs, out_shapes, operands


def _with_casts(body, n_in, n_out, n_cast):
    def kern(*refs):
        ins, rest = refs[:n_in], refs[n_in:]
        cast_in, rest = rest[:n_cast], rest[n_cast:]
        outs, rest = rest[:n_out], rest[n_out:]
        cast_out, scratch = rest[:n_cast], rest[n_cast:]
        body(*ins, *outs, *scratch)
        for src, dst in zip(cast_in, cast_out):
            dst[...] = src[0].astype(BF16)
    return kern


def _ada_kernel(cond_ref, w_ref, b_ref, o_ref):
    c = cond_ref[...]
    s = c * jax.nn.sigmoid(c)
    hi = s.astype(BF16)
    lo = (s - hi.astype(F32)).astype(BF16)
    w = w_ref[0].astype(BF16)
    o_ref[0] = _dot(hi, w) + _dot(lo, w) + b_ref[0]


def _ada(cond, ada_w, ada_b, bn=1536):
    depth, d, n = ada_w.shape
    return pl.pallas_call(
        _ada_kernel,
        grid=(depth, n // bn),
        in_specs=[
            pl.BlockSpec((HALO, d), lambda l, j: (0, 0)),
            pl.BlockSpec((1, d, bn), lambda l, j: (l, 0, j)),
            pl.BlockSpec((1, 1, bn), lambda l, j: (l, 0, j)),
        ],
        out_specs=pl.BlockSpec((1, HALO, bn), lambda l, j: (l, 0, j)),
        out_shape=jax.ShapeDtypeStruct((depth, HALO, n), F32),
        compiler_params=_params("parallel", "parallel"),
        name="ada",
    )(cond, ada_w, ada_b.reshape(depth, 1, n))


def _mod_rows(mod_ref, row, first):
    d = D_MODEL
    return [mod_ref[0, pl.ds(row, 1), (first + j) * d:(first + j + 1) * d] for j in range(3)]


def _layer_spec(a, layer):
    zeros = (0,) * (a.ndim - 1)
    return pl.BlockSpec((1,) + a.shape[1:], lambda *_: (layer,) + zeros, pipeline_mode=pl.Buffered(1))


def _decay_tables(ctx_len):
    c = RET_CHUNK
    h = np.arange(RET_HEADS, dtype=np.float64)
    lg_f = np.log1p(-np.exp2(-DECAY_EXP_FWD - h))[:, None]
    lg_b = np.log1p(-np.exp2(-DECAY_EXP_BWD - h))[:, None]
    idx = np.arange(c, dtype=np.float64)
    diff = idx[:, None] - idx[None, :]
    mask = (np.where(diff >= 0, np.exp(np.maximum(diff, 0.0)[None] * lg_f[:, :, None]), 0.0)
            + np.where(diff <= 0, np.exp(np.maximum(-diff, 0.0)[None] * lg_b[:, :, None]), 0.0))
    lane = lambda a: jnp.asarray(np.repeat(a[:, :, None], LANES, axis=2), F32)
    cidx = np.arange(ctx_len, dtype=np.float64)
    return dict(
        mask=jnp.asarray(mask, F32),
        qdec_f=lane(np.exp((idx[None] + 1.0) * lg_f)),
        kdec_f=lane(np.exp((c - 1.0 - idx[None]) * lg_f)),
        qdec_b=lane(np.exp((c - idx[None]) * lg_b)),
        kdec_b=lane(np.exp(idx[None] * lg_b)),
        ctxw_f=lane(np.exp((ctx_len - 1.0 - cidx[None]) * lg_f)),
        ctxw_b=lane(np.exp(cidx[None] * lg_b)),
        cdec_f=[float(v) for v in np.exp(c * lg_f[:, 0])],
        cdec_b=[float(v) for v in np.exp(c * lg_b[:, 0])],
    )


def _lanes(tab, n):
    return jnp.concatenate([tab] * (n // LANES), axis=1)


def _head(ref, rows, h, width):
    return ref[0, rows, h * width:(h + 1) * width]


def _scaled_bf16(k, tab):
    return (k.astype(F32) * _lanes(tab, k.shape[1])).astype(BF16)


def _init_state(s_ref, kc_ref, vc_ref, ctxw_ref):
    every = slice(None)
    for h in range(RET_HEADS):
        kw = _scaled_bf16(_head(kc_ref, every, h, RET_QK_DIM), ctxw_ref[h])
        s_ref[h] = _dot_tn(kw, _head(vc_ref, every, h, RET_V_DIM))


def _ctx_kv_kernel(x_ref, mod_ref, gain_ref, wk_ref, wv_ref, k_ref, v_ref):
    d = D_MODEL
    shift, scale, _ = _mod_rows(mod_ref, 2, 0)
    ax = _norm_mod(x_ref[0], gain_ref[0], shift, scale).astype(BF16)
    k_ref[0] = (_dot(ax, wk_ref[...]) * RET_QK_DIM ** -0.5).astype(BF16)
    for j in range(2):
        v_ref[0, :, j * d:(j + 1) * d] = _dot(ax, wv_ref[:, j * d:(j + 1) * d]).astype(BF16)


def _ctx_kv(ctx, mod, gain, w_qkvg, layer):
    bsz, t, d = ctx.shape
    row = lambda b: (b, 0, 0)
    return pl.pallas_call(
        _ctx_kv_kernel,
        grid=(bsz,),
        in_specs=[
            pl.BlockSpec((1, t, d), row),
            _layer_spec(mod, layer),
            _layer_spec(gain, layer),
            pl.BlockSpec((d, d), lambda b: (0, 1), pipeline_mode=pl.Buffered(1)),
            pl.BlockSpec((d, 2 * d), lambda b: (0, 1), pipeline_mode=pl.Buffered(1)),
        ],
        out_specs=[pl.BlockSpec((1, t, d), row), pl.BlockSpec((1, t, 2 * d), row)],
        out_shape=[
            jax.ShapeDtypeStruct((bsz, t, d), BF16),
            jax.ShapeDtypeStruct((bsz, t, 2 * d), BF16),
        ],
        compiler_params=_params("parallel"),
        name="ctx_kv",
    )(ctx, mod, gain, w_qkvg, w_qkvg)


def _rope_tile(rcos_ref, rsin_ref, ccos_ref, csin_ref, tile, tm):
    n_rows = tm // GRID_W
    row0 = tile * n_rows
    def build(r_ref, c_ref):
        rpart = jnp.concatenate(
            [jnp.broadcast_to(r_ref[pl.ds(row0 + r, 1), :], (GRID_W, LANES)) for r in range(n_rows)], axis=0)
        cpart = jnp.concatenate([c_ref[...]] * n_rows, axis=0)
        return jnp.concatenate([rpart, cpart], axis=1)
    return build(rcos_ref, ccos_ref), build(rsin_ref, csin_ref)


def _rope(r, cos, sin):
    outs = []
    for blk in range(r.shape[1] // LANES):
        xb = r[:, blk * LANES:(blk + 1) * LANES]
        tb = (blk % 2) * LANES
        outs.append(xb * cos[:, tb:tb + LANES]
                    + pltpu.roll(xb, LANES // 2, axis=1) * sin[:, tb:tb + LANES])
    return jnp.concatenate(outs, axis=1)


def _qkvg_kernel(x_ref, mod_ref, gain_ref, w_ref, rcos_ref, rsin_ref, ccos_ref, csin_ref,
                 kdec_ref, ctxw_ref, kc_ref, vc_ref, q_ref, k_ref, v_ref, g_ref, sb_ref, s_ref, *, cdec):
    d = D_MODEL
    tm = x_ref.shape[1]
    step = pl.program_id(1)
    tile = pl.num_programs(1) - 1 - step

    @pl.when(step == 0)
    def _():
        _init_state(s_ref, kc_ref, vc_ref, ctxw_ref)

    shift, scale, _ = _mod_rows(mod_ref, pl.program_id(0), 0)
    ax = _norm_mod(x_ref[0], gain_ref[0], shift, scale).astype(BF16)
    cos, sin = _rope_tile(rcos_ref, rsin_ref, ccos_ref, csin_ref, tile, tm)
    q_ref[0] = _rope(_dot(ax, w_ref[:, 0:d]), cos, sin).astype(BF16)
    kf = _rope(_dot(ax, w_ref[:, d:2 * d]), cos, sin) * RET_QK_DIM ** -0.5
    k_ref[0] = kf.astype(BF16)
    for j in range(2):
        v_ref[0, :, j * d:(j + 1) * d] = _dot(ax, w_ref[:, (2 + j) * d:(3 + j) * d]).astype(BF16)
        g_ref[0, :, j * d:(j + 1) * d] = _dot(ax, w_ref[:, (4 + j) * d:(5 + j) * d]).astype(BF16)

    for c in reversed(range(tm // RET_CHUNK)):
        rows = slice(c * RET_CHUNK, (c + 1) * RET_CHUNK)
        for h in range(RET_HEADS):
            s = s_ref[h]
            sb_ref[0, c, h] = s.astype(BF16)
            kd = (kf[rows, h * RET_QK_DIM:(h + 1) * RET_QK_DIM] * _lanes(kdec_ref[h], RET_QK_DIM)).astype(BF16)
            s_ref[h] = s * cdec[h] + _dot_tn(kd, _head(v_ref, rows, h, RET_V_DIM))


def _qkvg(x, mod, gain, w, kc, vc, tabs, rope, layer, casts, tm=512):
    bsz, t, d = x.shape
    n_t = t // tm
    grid = (bsz, n_t)
    row = lambda b, i: (b, n_t - 1 - i, 0)
    per_b = lambda b, i: (b, 0, 0)
    n_c = tm // RET_CHUNK
    sb_block = (1, n_c, RET_HEADS, RET_QK_DIM, RET_V_DIM)
    c_in, c_out, c_shapes, c_ops = _cast_plan(casts, grid)
    in_specs = [
        pl.BlockSpec((1, tm, d), row),
        _layer_spec(mod, layer),
        _layer_spec(gain, layer),
        _const_spec(w.shape),
        *[_const_spec(r.shape) for r in rope],
        _const_spec(tabs["kdec_b"].shape),
        _const_spec(tabs["ctxw_b"].shape),
        pl.BlockSpec((1,) + kc.shape[1:], per_b),
        pl.BlockSpec((1,) + vc.shape[1:], per_b),
    ]
    out_specs = [
        pl.BlockSpec((1, tm, d), row),
        pl.BlockSpec((1, tm, d), row),
        pl.BlockSpec((1, tm, 2 * d), row),
        pl.BlockSpec((1, tm, 2 * d), row),
        pl.BlockSpec(sb_block, lambda b, i: (b, n_t - 1 - i, 0, 0, 0)),
    ]
    out_shape = [
        jax.ShapeDtypeStruct((bsz, t, d), BF16),
        jax.ShapeDtypeStruct((bsz, t, d), BF16),
        jax.ShapeDtypeStruct((bsz, t, 2 * d), BF16),
        jax.ShapeDtypeStruct((bsz, t, 2 * d), BF16),
        jax.ShapeDtypeStruct((bsz, t // RET_CHUNK) + sb_block[2:], BF16),
    ]
    outs = pl.pallas_call(
        _with_casts(functools.partial(_qkvg_kernel, cdec=tabs["cdec_b"]), len(in_specs), len(out_specs), len(casts)),
        grid=grid,
        in_specs=in_specs + c_in,
        out_specs=out_specs + c_out,
        out_shape=out_shape + c_shapes,
        scratch_shapes=[pltpu.VMEM((RET_HEADS, RET_QK_DIM, RET_V_DIM), F32)],
        compiler_params=_params("arbitrary", "arbitrary"),
        name="qkvg",
    )(x, mod, gain, w, *rope, tabs["kdec_b"], tabs["ctxw_b"], kc, vc, *c_ops)
    return outs[:len(out_specs)], outs[len(out_specs):]


def _ret_kernel(q_ref, k_ref, v_ref, g_ref, sb_ref, x_ref, kc_ref, vc_ref, mask_ref, qdf_ref, kdf_ref,
                qdb_ref, ctxw_ref, mod_ref, wo_ref, o_ref, s_ref, y_ref, *, cdec):
    @pl.when(pl.program_id(1) == 0)
    def _():
        _init_state(s_ref, kc_ref, vc_ref, ctxw_ref)

    for c in range(x_ref.shape[1] // RET_CHUNK):
        rows = slice(c * RET_CHUNK, (c + 1) * RET_CHUNK)
        for h in range(RET_HEADS):
            q = _head(q_ref, rows, h, RET_QK_DIM)
            k = _head(k_ref, rows, h, RET_QK_DIM)
            v = _head(v_ref, rows, h, RET_V_DIM)
            p = (_dot_nt(q, k) * mask_ref[h]).astype(BF16)
            s = s_ref[h]
            o = (_dot(p, v)
                 + _dot(q, s.astype(BF16)) * _lanes(qdf_ref[h], RET_V_DIM)
                 + _dot(q, sb_ref[0, c, h]) * _lanes(qdb_ref[h], RET_V_DIM))
            o = o * lax.rsqrt(jnp.mean(o * o, axis=-1, keepdims=True) + EPS)
            g = _head(g_ref, rows, h, RET_V_DIM).astype(F32)
            y_ref[rows, h * RET_V_DIM:(h + 1) * RET_V_DIM] = (g * jax.nn.sigmoid(g) * o).astype(BF16)
            s_ref[h] = s * cdec[h] + _dot_tn(_scaled_bf16(k, kdf_ref[h]), v)

    gate = _mod_rows(mod_ref, pl.program_id(0), 0)[2]
    o_ref[0] = x_ref[0] + gate * _dot(y_ref[...], wo_ref[...])


def _retention(x, q, k, v, g, sb, kc, vc, mod, w_o, tabs, layer, blk=512):
    bsz, t, d = x.shape
    row = lambda b, i: (b, i, 0)
    per_b = lambda b, i: (b, 0, 0)
    sb_block = (1, blk // RET_CHUNK) + sb.shape[2:]
    consts = [tabs["mask"], tabs["qdec_f"], tabs["kdec_f"], tabs["qdec_b"], tabs["ctxw_f"]]
    return pl.pallas_call(
        functools.partial(_ret_kernel, cdec=tabs["cdec_f"]),
        grid=(bsz, t // blk),
        in_specs=[
            pl.BlockSpec((1, blk, d), row),
            pl.BlockSpec((1, blk, d), row),
            pl.BlockSpec((1, blk, 2 * d), row),
            pl.BlockSpec((1, blk, 2 * d), row),
            pl.BlockSpec(sb_block, lambda b, i: (b, i, 0, 0, 0)),
            pl.BlockSpec((1, blk, d), row),
            pl.BlockSpec((1,) + kc.shape[1:], per_b),
            pl.BlockSpec((1,) + vc.shape[1:], per_b),
            *[_const_spec(a.shape) for a in consts],
            _layer_spec(mod, layer),
            _const_spec(w_o.shape),
        ],
        out_specs=pl.BlockSpec((1, blk, d), row),
        out_shape=jax.ShapeDtypeStruct((bsz, t, d), F32),
        scratch_shapes=[pltpu.VMEM((RET_HEADS, RET_QK_DIM, RET_V_DIM), F32), pltpu.VMEM((blk, 2 * d), BF16)],
        compiler_params=_params("parallel", "arbitrary"),
        name="ret",
    )(q, k, v, g, sb, x, kc, vc, *consts, mod, w_o)


def _ffn_chunks(f):
    step = 768
    return [(c, min(c + step, f)) for c in range(0, f, step)]


def _ffn_kernel(x_ref, mod_ref, gain_ref, w1_ref, w3_ref, w2_ref, fgain_ref, o_ref, *, final_norm):
    shift, scale, gate = _mod_rows(mod_ref, pl.program_id(0), 3)
    x = x_ref[0]
    fx = _norm_mod(x, gain_ref[0], shift, scale).astype(BF16)
    acc = None
    for c0, c1 in _ffn_chunks(w1_ref.shape[1]):
        a = _dot(fx, w1_ref[:, c0:c1])
        hid = (a * jax.nn.sigmoid(a) * _dot(fx, w3_ref[:, c0:c1])).astype(BF16)
        part = _dot(hid, w2_ref[c0:c1, :])
        acc = part if acc is None else acc + part
    out = x + gate * acc
    if final_norm:
        out = out * lax.rsqrt(jnp.mean(out * out, axis=-1, keepdims=True) + EPS) * fgain_ref[...]
    o_ref[0] = out


def _ffn(x, mod, gain, w1, w3, w2, fgain, layer, final_norm, casts=(), tm=512):
    bsz, t, d = x.shape
    grid = (bsz, t // tm)
    row = lambda b, i: (b, i, 0)
    c_in, c_out, c_shapes, c_ops = _cast_plan(casts, grid)
    in_specs = [
        pl.BlockSpec((1, tm, d), row),
        _layer_spec(mod, layer),
        _layer_spec(gain, layer),
        _const_spec(w1.shape),
        _const_spec(w3.shape),
        _const_spec(w2.shape),
        _const_spec(fgain.shape),
    ]
    outs = pl.pallas_call(
        _with_casts(functools.partial(_ffn_kernel, final_norm=final_norm), len(in_specs), 1, len(casts)),
        grid=grid,
        in_specs=in_specs + c_in,
        out_specs=[pl.BlockSpec((1, tm, d), row)] + c_out,
        out_shape=[jax.ShapeDtypeStruct((bsz, t, d), F32)] + c_shapes,
        compiler_params=_params("arbitrary", "arbitrary"),
        name="ffn_final" if final_norm else "ffn",
    )(x, mod, gain, w1, w3, w2, fgain, *c_ops)
    return outs[0], outs[1:]


def _conv_kernel(x_ref, prev_ref, next_ref, mod_ref, gain_ref, win_ref, cw_ref, wout_ref, o_ref, u_ref):
    d = D_MODEL
    tm = x_ref.shape[1]
    i, last = pl.program_id(1), pl.num_programs(1) - 1
    shift, scale, gate = _mod_rows(mod_ref, pl.program_id(0), 0)
    x = x_ref[0]
    halo = jnp.concatenate([prev_ref[0], next_ref[0]], axis=0)
    ax = jnp.concatenate([_norm_mod(x, gain_ref[0], shift, scale),
                          _norm_mod(halo, gain_ref[0], shift, scale)], axis=0).astype(BF16)
    u = _dot(ax, win_ref[:, d:2 * d]) * _dot(ax, win_ref[:, 2 * d:3 * d])
    u_ref[0:HALO] = u[tm:tm + HALO] * jnp.where(i == 0, 0.0, 1.0)
    u_ref[HALO:HALO + tm] = u[0:tm]
    u_ref[HALO + tm:] = u[tm + HALO:] * jnp.where(i == last, 0.0, 1.0)
    cw = cw_ref[0]
    conv = (u_ref[pl.ds(HALO - 1, tm)] * cw[0:1] + u[0:tm] * cw[1:2]
            + u_ref[pl.ds(HALO + 1, tm)] * cw[2:3])
    b_gate = _dot(ax[0:tm], win_ref[:, 0:d])
    o_ref[0] = x + gate * _dot((b_gate * conv).astype(BF16), wout_ref[...])


def _conv(x, mod, gain, w_in, conv_w, w_out, layer, tm=512):
    bsz, t, d = x.shape
    per = tm // HALO
    n_halo = t // HALO
    row = lambda b, i: (b, i, 0)
    return pl.pallas_call(
        _conv_kernel,
        grid=(bsz, t // tm),
        in_specs=[
            pl.BlockSpec((1, tm, d), row),
            pl.BlockSpec((1, HALO, d), lambda b, i: (b, jnp.maximum(i * per - 1, 0), 0)),
            pl.BlockSpec((1, HALO, d), lambda b, i: (b, jnp.minimum((i + 1) * per, n_halo - 1), 0)),
            _layer_spec(mod, layer),
            _layer_spec(gain, layer),
            _const_spec(w_in.shape),
            _layer_spec(conv_w, 0),
            _const_spec(w_out.shape),
        ],
        out_specs=pl.BlockSpec((1, tm, d), row),
        out_shape=jax.ShapeDtypeStruct((bsz, t, d), F32),
        scratch_shapes=[pltpu.VMEM((tm + 2 * HALO, d), F32)],
        compiler_params=_params("parallel", "parallel"),
        name="conv",
    )(x, x, x, mod, gain, w_in, conv_w, w_out)


def _rope_tables(t):
    n = LANES // 2
    freqs = ROPE_BASE ** (-np.arange(n, dtype=np.float64) / n)
    def tabs(count):
        ang = np.arange(count, dtype=np.float64)[:, None] * freqs[None, :]
        return (jnp.asarray(np.concatenate([np.cos(ang), np.cos(ang)], axis=1), F32),
                jnp.asarray(np.concatenate([-np.sin(ang), np.sin(ang)], axis=1), F32))
    rcos, rsin = tabs(t // GRID_W)
    ccos, csin = tabs(GRID_W)
    return rcos, rsin, ccos, csin


def kernel(x, c, ctx, c_ctx, ada_w, ada_b, norm_mix, norm_ffn, ret_w_qkvg, ret_w_o, conv_w_in, conv_w,
           conv_w_out, ffn_w1, ffn_w3, ffn_w2, final_norm):
    bsz, t, d = x.shape
    cond = jnp.concatenate([c, c_ctx[None], jnp.zeros((HALO - bsz - 1, d), F32)], axis=0)
    mod = _ada(cond, ada_w, ada_b)
    depth = mod.shape[0]
    gain_mix = norm_mix.reshape(depth, 1, d)
    gain_ffn = norm_ffn.reshape(depth, 1, d)
    fgain = final_norm.reshape(1, d)
    tabs = _decay_tables(ctx.shape[1])

    w_qkvg = ret_w_qkvg[0].astype(BF16)
    kc, vc = _ctx_kv(ctx, mod, gain_mix, w_qkvg, 0)
    (q, k, v, g, sb), (w_o, w1, w3, w2) = _qkvg(
        x, mod, gain_mix, w_qkvg, kc, vc, tabs, _rope_tables(t), 0,
        casts=[(ret_w_o, 0), (ffn_w1, 0), (ffn_w3, 0), (ffn_w2, 0)])
    hx = _retention(x, q, k, v, g, sb, kc, vc, mod, w_o, tabs, 0)
    hx, (w_in, w_out, w1, w3, w2) = _ffn(
        hx, mod, gain_ffn, w1, w3, w2, fgain, 0, final_norm=False,
        casts=[(conv_w_in, 0), (conv_w_out, 0), (ffn_w1, 1), (ffn_w3, 1), (ffn_w2, 1)])

    hx = _conv(hx, mod, gain_mix, w_in, conv_w, w_out, 1)
    return _ffn(hx, mod, gain_ffn, w1, w3, w2, fgain, 1, final_norm=True)[0]
```

```python
import functools

import jax
import jax.numpy as jnp
import numpy as np
from jax import lax
from jax.experimental import pallas as pl
from jax.experimental.pallas import tpu as pltpu

D_MODEL = 1024
GRID_W = 64
RET_HEADS = 4
RET_QK_DIM = D_MODEL // RET_HEADS
RET_V_DIM = 2 * D_MODEL // RET_HEADS
DECAY_EXP_FWD = 5.0
DECAY_EXP_BWD = 5.5
ROPE_BASE = 10000.0
EPS = 1e-6

RET_CHUNK = 256
LANES = 128
HALO = 8
VMEM_LIMIT = 56 * 1024 * 1024

BF16 = jnp.bfloat16
F32 = jnp.float32


def _const_spec(shape):
    return pl.BlockSpec(shape, lambda *_: (0,) * len(shape), pipeline_mode=pl.Buffered(1))


def _params(*sem):
    return pltpu.CompilerParams(dimension_semantics=sem, vmem_limit_bytes=VMEM_LIMIT)


def _norm_mod(x, gain, shift, scale):
    y = x * lax.rsqrt(jnp.mean(x * x, axis=-1, keepdims=True) + EPS) * gain
    return y * (1.0 + scale) + shift


def _dot(a, b):
    return jnp.dot(a, b, preferred_element_type=F32)


def _dot_nt(a, b):
    return lax.dot_general(a, b, (((1,), (1,)), ((), ())), preferred_element_type=F32)


def _dot_tn(a, b):
    return lax.dot_general(a, b, (((0,), (0,)), ((), ())), preferred_element_type=F32)


BF16_ROWS = 16


def _cast_plan(weights, grid):
    n_steps = grid[0] * grid[1]
    in_specs, out_specs, out_shapes, operands = [], [], [], []
    for w, layer in weights:
        _, r, c = w.shape
        n_blk = max(n for n in range(1, n_steps + 1) if r % n == 0 and (r // n) % BF16_ROWS == 0)
        blk = lambda b, i, n_blk=n_blk: jnp.minimum(b * grid[1] + i, n_blk - 1)
        in_specs.append(pl.BlockSpec((1, r // n_blk, c), lambda b, i, blk=blk, layer=layer: (layer, blk(b, i), 0)))
        out_specs.append(pl.BlockSpec((r // n_blk, c), lambda b, i, blk=blk: (blk(b, i), 0)))
        out_shapes.append(jax.ShapeDtypeStruct((r, c), BF16))
        operands.append(w)
    return in_specs, out_specs, out_shapes, operands


def _with_casts(body, n_in, n_out, n_cast):
    def kern(*refs):
        ins, rest = refs[:n_in], refs[n_in:]
        cast_in, rest = rest[:n_cast], rest[n_cast:]
        outs, rest = rest[:n_out], rest[n_out:]
        cast_out, scratch = rest[:n_cast], rest[n_cast:]
        body(*ins, *outs, *scratch)
        for src, dst in zip(cast_in, cast_out):
            dst[...] = src[0].astype(BF16)
    return kern


def _ada_kernel(cond_ref, w_ref, b_ref, o_ref):
    c = cond_ref[...]
    s = c * jax.nn.sigmoid(c)
    hi = s.astype(BF16)
    lo = (s - hi.astype(F32)).astype(BF16)
    w = w_ref[0].astype(BF16)
    o_ref[0] = _dot(hi, w) + _dot(lo, w) + b_ref[0]


def _ada(cond, ada_w, ada_b, bn=1536):
    depth, d, n = ada_w.shape
    return pl.pallas_call(
        _ada_kernel,
        grid=(depth, n // bn),
        in_specs=[
            pl.BlockSpec((HALO, d), lambda l, j: (0, 0)),
            pl.BlockSpec((1, d, bn), lambda l, j: (l, 0, j)),
            pl.BlockSpec((1, 1, bn), lambda l, j: (l, 0, j)),
        ],
        out_specs=pl.BlockSpec((1, HALO, bn), lambda l, j: (l, 0, j)),
        out_shape=jax.ShapeDtypeStruct((depth, HALO, n), F32),
        compiler_params=_params("parallel", "parallel"),
        name="ada",
    )(cond, ada_w, ada_b.reshape(depth, 1, n))


def _mod_rows(mod_ref, row, first):
    d = D_MODEL
    return [mod_ref[0, pl.ds(row, 1), (first + j) * d:(first + j + 1) * d] for j in range(3)]


def _layer_spec(a, layer):
    zeros = (0,) * (a.ndim - 1)
    return pl.BlockSpec((1,) + a.shape[1:], lambda *_: (layer,) + zeros, pipeline_mode=pl.Buffered(1))


def _decay_tables(ctx_len):
    c = RET_CHUNK
    h = np.arange(RET_HEADS, dtype=np.float64)
    lg_f = np.log1p(-np.exp2(-DECAY_EXP_FWD - h))[:, None]
    lg_b = np.log1p(-np.exp2(-DECAY_EXP_BWD - h))[:, None]
    idx = np.arange(c, dtype=np.float64)
    diff = idx[:, None] - idx[None, :]
    mask = (np.where(diff >= 0, np.exp(np.maximum(diff, 0.0)[None] * lg_f[:, :, None]), 0.0)
            + np.where(diff <= 0, np.exp(np.maximum(-diff, 0.0)[None] * lg_b[:, :, None]), 0.0))
    lane = lambda a: jnp.asarray(np.repeat(a[:, :, None], LANES, axis=2), F32)
    cidx = np.arange(ctx_len, dtype=np.float64)
    return dict(
        mask=jnp.asarray(mask, F32),
        qdec_f=lane(np.exp((idx[None] + 1.0) * lg_f)),
        kdec_f=lane(np.exp((c - 1.0 - idx[None]) * lg_f)),
        qdec_b=lane(np.exp((c - idx[None]) * lg_b)),
        kdec_b=lane(np.exp(idx[None] * lg_b)),
        ctxw_f=lane(np.exp((ctx_len - 1.0 - cidx[None]) * lg_f)),
        ctxw_b=lane(np.exp(cidx[None] * lg_b)),
        cdec_f=[float(v) for v in np.exp(c * lg_f[:, 0])],
        cdec_b=[float(v) for v in np.exp(c * lg_b[:, 0])],
    )


def _lanes(tab, n):
    return jnp.concatenate([tab] * (n // LANES), axis=1)


def _head(ref, rows, h, width):
    return ref[0, rows, h * width:(h + 1) * width]


def _scaled_bf16(k, tab):
    return (k.astype(F32) * _lanes(tab, k.shape[1])).astype(BF16)


def _init_state(s_ref, kc_ref, vc_ref, ctxw_ref):
    every = slice(None)
    for h in range(RET_HEADS):
        kw = _scaled_bf16(_head(kc_ref, every, h, RET_QK_DIM), ctxw_ref[h])
        s_ref[h] = _dot_tn(kw, _head(vc_ref, every, h, RET_V_DIM))


def _ctx_kv_kernel(x_ref, mod_ref, gain_ref, wk_ref, wv_ref, k_ref, v_ref):
    d = D_MODEL
    shift, scale, _ = _mod_rows(mod_ref, 2, 0)
    ax = _norm_mod(x_ref[0], gain_ref[0], shift, scale).astype(BF16)
    k_ref[0] = (_dot(ax, wk_ref[...]) * RET_QK_DIM ** -0.5).astype(BF16)
    for j in range(2):
        v_ref[0, :, j * d:(j + 1) * d] = _dot(ax, wv_ref[:, j * d:(j + 1) * d]).astype(BF16)


def _ctx_kv(ctx, mod, gain, w_qkvg, layer):
    bsz, t, d = ctx.shape
    row = lambda b: (b, 0, 0)
    return pl.pallas_call(
        _ctx_kv_kernel,
        grid=(bsz,),
        in_specs=[
            pl.BlockSpec((1, t, d), row),
            _layer_spec(mod, layer),
            _layer_spec(gain, layer),
            pl.BlockSpec((d, d), lambda b: (0, 1), pipeline_mode=pl.Buffered(1)),
            pl.BlockSpec((d, 2 * d), lambda b: (0, 1), pipeline_mode=pl.Buffered(1)),
        ],
        out_specs=[pl.BlockSpec((1, t, d), row), pl.BlockSpec((1, t, 2 * d), row)],
        out_shape=[
            jax.ShapeDtypeStruct((bsz, t, d), BF16),
            jax.ShapeDtypeStruct((bsz, t, 2 * d), BF16),
        ],
        compiler_params=_params("parallel"),
        name="ctx_kv",
    )(ctx, mod, gain, w_qkvg, w_qkvg)


def _rope_tile(rcos_ref, rsin_ref, ccos_ref, csin_ref, tile, tm):
    n_rows = tm // GRID_W
    row0 = tile * n_rows
    def build(r_ref, c_ref):
        rpart = jnp.concatenate(
            [jnp.broadcast_to(r_ref[pl.ds(row0 + r, 1), :], (GRID_W, LANES)) for r in range(n_rows)], axis=0)
        cpart = jnp.concatenate([c_ref[...]] * n_rows, axis=0)
        return jnp.concatenate([rpart, cpart], axis=1)
    return build(rcos_ref, ccos_ref), build(rsin_ref, csin_ref)


def _rope(r, cos, sin):
    outs = []
    for blk in range(r.shape[1] // LANES):
        xb = r[:, blk * LANES:(blk + 1) * LANES]
        tb = (blk % 2) * LANES
        outs.append(xb * cos[:, tb:tb + LANES]
                    + pltpu.roll(xb, LANES // 2, axis=1) * sin[:, tb:tb + LANES])
    return jnp.concatenate(outs, axis=1)


def _qkvg_kernel(x_ref, mod_ref, gain_ref, w_ref, rcos_ref, rsin_ref, ccos_ref, csin_ref,
                 kdec_ref, ctxw_ref, kc_ref, vc_ref, q_ref, k_ref, v_ref, g_ref, sb_ref, s_ref, *, cdec):
    d = D_MODEL
    tm = x_ref.shape[1]
    step = pl.program_id(1)
    tile = pl.num_programs(1) - 1 - step

    @pl.when(step == 0)
    def _():
        _init_state(s_ref, kc_ref, vc_ref, ctxw_ref)

    shift, scale, _ = _mod_rows(mod_ref, pl.program_id(0), 0)
    ax = _norm_mod(x_ref[0], gain_ref[0], shift, scale).astype(BF16)
    cos, sin = _rope_tile(rcos_ref, rsin_ref, ccos_ref, csin_ref, tile, tm)
    q_ref[0] = _rope(_dot(ax, w_ref[:, 0:d]), cos, sin).astype(BF16)
    kf = _rope(_dot(ax, w_ref[:, d:2 * d]), cos, sin) * RET_QK_DIM ** -0.5
    k_ref[0] = kf.astype(BF16)
    for j in range(2):
        v_ref[0, :, j * d:(j + 1) * d] = _dot(ax, w_ref[:, (2 + j) * d:(3 + j) * d]).astype(BF16)
        g_ref[0, :, j * d:(j + 1) * d] = _dot(ax, w_ref[:, (4 + j) * d:(5 + j) * d]).astype(BF16)

    for c in reversed(range(tm // RET_CHUNK)):
        rows = slice(c * RET_CHUNK, (c + 1) * RET_CHUNK)
        for h in range(RET_HEADS):
            s = s_ref[h]
            sb_ref[0, c, h] = s.astype(BF16)
            kd = (kf[rows, h * RET_QK_DIM:(h + 1) * RET_QK_DIM] * _lanes(kdec_ref[h], RET_QK_DIM)).astype(BF16)
            s_ref[h] = s * cdec[h] + _dot_tn(kd, _head(v_ref, rows, h, RET_V_DIM))


def _qkvg(x, mod, gain, w, kc, vc, tabs, rope, layer, casts, tm=512):
    bsz, t, d = x.shape
    n_t = t // tm
    grid = (bsz, n_t)
    row = lambda b, i: (b, n_t - 1 - i, 0)
    per_b = lambda b, i: (b, 0, 0)
    n_c = tm // RET_CHUNK
    sb_block = (1, n_c, RET_HEADS, RET_QK_DIM, RET_V_DIM)
    c_in, c_out, c_shapes, c_ops = _cast_plan(casts, grid)
    in_specs = [
        pl.BlockSpec((1, tm, d), row),
        _layer_spec(mod, layer),
        _layer_spec(gain, layer),
        _const_spec(w.shape),
        *[_const_spec(r.shape) for r in rope],
        _const_spec(tabs["kdec_b"].shape),
        _const_spec(tabs["ctxw_b"].shape),
        pl.BlockSpec((1,) + kc.shape[1:], per_b),
        pl.BlockSpec((1,) + vc.shape[1:], per_b),
    ]
    out_specs = [
        pl.BlockSpec((1, tm, d), row),
        pl.BlockSpec((1, tm, d), row),
        pl.BlockSpec((1, tm, 2 * d), row),
        pl.BlockSpec((1, tm, 2 * d), row),
        pl.BlockSpec(sb_block, lambda b, i: (b, n_t - 1 - i, 0, 0, 0)),
    ]
    out_shape = [
        jax.ShapeDtypeStruct((bsz, t, d), BF16),
        jax.ShapeDtypeStruct((bsz, t, d), BF16),
        jax.ShapeDtypeStruct((bsz, t, 2 * d), BF16),
        jax.ShapeDtypeStruct((bsz, t, 2 * d), BF16),
        jax.ShapeDtypeStruct((bsz, t // RET_CHUNK) + sb_block[2:], BF16),
    ]
    outs = pl.pallas_call(
        _with_casts(functools.partial(_qkvg_kernel, cdec=tabs["cdec_b"]), len(in_specs), len(out_specs), len(casts)),
        grid=grid,
        in_specs=in_specs + c_in,
        out_specs=out_specs + c_out,
        out_shape=out_shape + c_shapes,
        scratch_shapes=[pltpu.VMEM((RET_HEADS, RET_QK_DIM, RET_V_DIM), F32)],
        compiler_params=_params("arbitrary", "arbitrary"),
        name="qkvg",
    )(x, mod, gain, w, *rope, tabs["kdec_b"], tabs["ctxw_b"], kc, vc, *c_ops)
    return outs[:len(out_specs)], outs[len(out_specs):]


def _ret_kernel(q_ref, k_ref, v_ref, g_ref, sb_ref, x_ref, kc_ref, vc_ref, mask_ref, qdf_ref, kdf_ref,
                qdb_ref, ctxw_ref, mod_ref, wo_ref, o_ref, s_ref, y_ref, *, cdec):
    @pl.when(pl.program_id(1) == 0)
    def _():
        _init_state(s_ref, kc_ref, vc_ref, ctxw_ref)

    for c in range(x_ref.shape[1] // RET_CHUNK):
        rows = slice(c * RET_CHUNK, (c + 1) * RET_CHUNK)
        for h in range(RET_HEADS):
            q = _head(q_ref, rows, h, RET_QK_DIM)
            k = _head(k_ref, rows, h, RET_QK_DIM)
            v = _head(v_ref, rows, h, RET_V_DIM)
            p = (_dot_nt(q, k) * mask_ref[h]).astype(BF16)
            s = s_ref[h]
            o = (_dot(p, v)
                 + _dot(q, s.astype(BF16)) * _lanes(qdf_ref[h], RET_V_DIM)
                 + _dot(q, sb_ref[0, c, h]) * _lanes(qdb_ref[h], RET_V_DIM))
            o = o * lax.rsqrt(jnp.mean(o * o, axis=-1, keepdims=True) + EPS)
            g = _head(g_ref, rows, h, RET_V_DIM).astype(F32)
            y_ref[rows, h * RET_V_DIM:(h + 1) * RET_V_DIM] = (g * jax.nn.sigmoid(g) * o).astype(BF16)
            s_ref[h] = s * cdec[h] + _dot_tn(_scaled_bf16(k, kdf_ref[h]), v)

    gate = _mod_rows(mod_ref, pl.program_id(0), 0)[2]
    o_ref[0] = x_ref[0] + gate * _dot(y_ref[...], wo_ref[...])


def _retention(x, q, k, v, g, sb, kc, vc, mod, w_o, tabs, layer, blk=512):
    bsz, t, d = x.shape
    row = lambda b, i: (b, i, 0)
    per_b = lambda b, i: (b, 0, 0)
    sb_block = (1, blk // RET_CHUNK) + sb.shape[2:]
    consts = [tabs["mask"], tabs["qdec_f"], tabs["kdec_f"], tabs["qdec_b"], tabs["ctxw_f"]]
    return pl.pallas_call(
        functools.partial(_ret_kernel, cdec=tabs["cdec_f"]),
        grid=(bsz, t // blk),
        in_specs=[
            pl.BlockSpec((1, blk, d), row),
            pl.BlockSpec((1, blk, d), row),
            pl.BlockSpec((1, blk, 2 * d), row),
            pl.BlockSpec((1, blk, 2 * d), row),
            pl.BlockSpec(sb_block, lambda b, i: (b, i, 0, 0, 0)),
            pl.BlockSpec((1, blk, d), row),
            pl.BlockSpec((1,) + kc.shape[1:], per_b),
            pl.BlockSpec((1,) + vc.shape[1:], per_b),
            *[_const_spec(a.shape) for a in consts],
            _layer_spec(mod, layer),
            _const_spec(w_o.shape),
        ],
        out_specs=pl.BlockSpec((1, blk, d), row),
        out_shape=jax.ShapeDtypeStruct((bsz, t, d), F32),
        scratch_shapes=[pltpu.VMEM((RET_HEADS, RET_QK_DIM, RET_V_DIM), F32), pltpu.VMEM((blk, 2 * d), BF16)],
        compiler_params=_params("parallel", "arbitrary"),
        name="ret",
    )(q, k, v, g, sb, x, kc, vc, *consts, mod, w_o)


def _ffn_chunks(f):
    step = 768
    return [(c, min(c + step, f)) for c in range(0, f, step)]


def _ffn_kernel(x_ref, mod_ref, gain_ref, w1_ref, w3_ref, w2_ref, fgain_ref, o_ref, *hid_ref, final_norm):
    shift, scale, gate = _mod_rows(mod_ref, pl.program_id(0), 3)
    x = x_ref[0]
    fx = _norm_mod(x, gain_ref[0], shift, scale).astype(BF16)
    acc = None
    for c0, c1 in _ffn_chunks(w1_ref.shape[1]):
        a = _dot(fx, w1_ref[:, c0:c1])
        hid = (a * jax.nn.sigmoid(a) * _dot(fx, w3_ref[:, c0:c1])).astype(BF16)
        if hid_ref:
            hid_ref[0][:, c0:c1] = hid
        else:
            part = _dot(hid, w2_ref[c0:c1, :])
            acc = part if acc is None else acc + part
    if hid_ref:
        acc = _dot(hid_ref[0][...], w2_ref[...])
    out = x + gate * acc
    if final_norm:
        out = out * lax.rsqrt(jnp.mean(out * out, axis=-1, keepdims=True) + EPS) * fgain_ref[...]
    o_ref[0] = out


def _ffn(x, mod, gain, w1, w3, w2, fgain, layer, final_norm, casts=(), tm=512):
    bsz, t, d = x.shape
    grid = (bsz, t // tm)
    row = lambda b, i: (b, i, 0)
    c_in, c_out, c_shapes, c_ops = _cast_plan(casts, grid)
    in_specs = [
        pl.BlockSpec((1, tm, d), row),
        _layer_spec(mod, layer),
        _layer_spec(gain, layer),
        _const_spec(w1.shape),
        _const_spec(w3.shape),
        _const_spec(w2.shape),
        _const_spec(fgain.shape),
    ]
    outs = pl.pallas_call(
        _with_casts(functools.partial(_ffn_kernel, final_norm=final_norm), len(in_specs), 1, len(casts)),
        grid=grid,
        in_specs=in_specs + c_in,
        out_specs=[pl.BlockSpec((1, tm, d), row)] + c_out,
        out_shape=[jax.ShapeDtypeStruct((bsz, t, d), F32)] + c_shapes,
        scratch_shapes=[] if final_norm else [pltpu.VMEM((tm, w1.shape[1]), BF16)],
        compiler_params=_params("arbitrary", "arbitrary"),
        name="ffn_final" if final_norm else "ffn",
    )(x, mod, gain, w1, w3, w2, fgain, *c_ops)
    return outs[0], outs[1:]


def _conv_kernel(x_ref, prev_ref, next_ref, mod_ref, gain_ref, win_ref, cw_ref, wout_ref, o_ref, u_ref):
    d = D_MODEL
    tm = x_ref.shape[1]
    i, last = pl.program_id(1), pl.num_programs(1) - 1
    shift, scale, gate = _mod_rows(mod_ref, pl.program_id(0), 0)
    x = x_ref[0]
    halo = jnp.concatenate([prev_ref[0], next_ref[0]], axis=0)
    ax = jnp.concatenate([_norm_mod(x, gain_ref[0], shift, scale),
                          _norm_mod(halo, gain_ref[0], shift, scale)], axis=0).astype(BF16)
    u = _dot(ax, win_ref[:, d:2 * d]) * _dot(ax, win_ref[:, 2 * d:3 * d])
    u_ref[0:HALO] = u[tm:tm + HALO] * jnp.where(i == 0, 0.0, 1.0)
    u_ref[HALO:HALO + tm] = u[0:tm]
    u_ref[HALO + tm:] = u[tm + HALO:] * jnp.where(i == last, 0.0, 1.0)
    cw = cw_ref[0]
    conv = (u_ref[pl.ds(HALO - 1, tm)] * cw[0:1] + u[0:tm] * cw[1:2]
            + u_ref[pl.ds(HALO + 1, tm)] * cw[2:3])
    b_gate = _dot(ax[0:tm], win_ref[:, 0:d])
    o_ref[0] = x + gate * _dot((b_gate * conv).astype(BF16), wout_ref[...])


def _conv(x, mod, gain, w_in, conv_w, w_out, layer, tm=1024):
    bsz, t, d = x.shape
    per = tm // HALO
    n_halo = t // HALO
    row = lambda b, i: (b, i, 0)
    return pl.pallas_call(
        _conv_kernel,
        grid=(bsz, t // tm),
        in_specs=[
            pl.BlockSpec((1, tm, d), row),
            pl.BlockSpec((1, HALO, d), lambda b, i: (b, jnp.maximum(i * per - 1, 0), 0)),
            pl.BlockSpec((1, HALO, d), lambda b, i: (b, jnp.minimum((i + 1) * per, n_halo - 1), 0)),
            _layer_spec(mod, layer),
            _layer_spec(gain, layer),
            _const_spec(w_in.shape),
            _layer_spec(conv_w, 0),
            _const_spec(w_out.shape),
        ],
        out_specs=pl.BlockSpec((1, tm, d), row),
        out_shape=jax.ShapeDtypeStruct((bsz, t, d), F32),
        scratch_shapes=[pltpu.VMEM((tm + 2 * HALO, d), F32)],
        compiler_params=_params("parallel", "parallel"),
        name="conv",
    )(x, x, x, mod, gain, w_in, conv_w, w_out)


def _rope_tables(t):
    n = LANES // 2
    freqs = ROPE_BASE ** (-np.arange(n, dtype=np.float64) / n)
    def tabs(count):
        ang = np.arange(count, dtype=np.float64)[:, None] * freqs[None, :]
        return (jnp.asarray(np.concatenate([np.cos(ang), np.cos(ang)], axis=1), F32),
                jnp.asarray(np.concatenate([-np.sin(ang), np.sin(ang)], axis=1), F32))
    rcos, rsin = tabs(t // GRID_W)
    ccos, csin = tabs(GRID_W)
    return rcos, rsin, ccos, csin


def kernel(x, c, ctx, c_ctx, ada_w, ada_b, norm_mix, norm_ffn, ret_w_qkvg, ret_w_o, conv_w_in, conv_w,
           conv_w_out, ffn_w1, ffn_w3, ffn_w2, final_norm):
    bsz, t, d = x.shape
    cond = jnp.concatenate([c, c_ctx[None], jnp.zeros((HALO - bsz - 1, d), F32)], axis=0)
    mod = _ada(cond, ada_w, ada_b)
    depth = mod.shape[0]
    gain_mix = norm_mix.reshape(depth, 1, d)
    gain_ffn = norm_ffn.reshape(depth, 1, d)
    fgain = final_norm.reshape(1, d)
    tabs = _decay_tables(ctx.shape[1])

    w_qkvg = ret_w_qkvg[0].astype(BF16)
    kc, vc = _ctx_kv(ctx, mod, gain_mix, w_qkvg, 0)
    (q, k, v, g, sb), (w_o, w1, w3, w2) = _qkvg(
        x, mod, gain_mix, w_qkvg, kc, vc, tabs, _rope_tables(t), 0,
        casts=[(ret_w_o, 0), (ffn_w1, 0), (ffn_w3, 0), (ffn_w2, 0)])
    hx = _retention(x, q, k, v, g, sb, kc, vc, mod, w_o, tabs, 0)
    hx, (w_in, w_out, w1, w3, w2) = _ffn(
        hx, mod, gain_ffn, w1, w3, w2, fgain, 0, final_norm=False,
        casts=[(conv_w_in, 0), (conv_w_out, 0), (ffn_w1, 1), (ffn_w3, 1), (ffn_w2, 1)])

    hx = _conv(hx, mod, gain_mix, w_in, conv_w, w_out, 1)
    return _ffn(hx, mod, gain_ffn, w1, w3, w2, fgain, 1, final_norm=True)[0]
```

```python
import functools

import jax
import jax.numpy as jnp
import numpy as np
from jax import lax
from jax.experimental import pallas as pl
from jax.experimental.pallas import tpu as pltpu

D_MODEL = 1024
GRID_W = 64
RET_HEADS = 4
RET_QK_DIM = D_MODEL // RET_HEADS
RET_V_DIM = 2 * D_MODEL // RET_HEADS
DECAY_EXP_FWD = 5.0
DECAY_EXP_BWD = 5.5
ROPE_BASE = 10000.0
EPS = 1e-6

RET_CHUNK = 256
LANES = 128
HALO = 8
VMEM_LIMIT = 56 * 1024 * 1024

BF16 = jnp.bfloat16
F32 = jnp.float32


def _const_spec(shape):
    return pl.BlockSpec(shape, lambda *_: (0,) * len(shape), pipeline_mode=pl.Buffered(1))


def _params(*sem):
    return pltpu.CompilerParams(dimension_semantics=sem, vmem_limit_bytes=VMEM_LIMIT)


def _norm_mod(x, gain, shift, scale):
    y = x * lax.rsqrt(jnp.mean(x * x, axis=-1, keepdims=True) + EPS) * gain
    return y * (1.0 + scale) + shift


def _dot(a, b):
    return jnp.dot(a, b, preferred_element_type=F32)


def _dot_nt(a, b):
    return lax.dot_general(a, b, (((1,), (1,)), ((), ())), preferred_element_type=F32)


def _dot_tn(a, b):
    return lax.dot_general(a, b, (((0,), (0,)), ((), ())), preferred_element_type=F32)


BF16_ROWS = 16


def _cast_plan(weights, grid):
    n_steps = grid[0] * grid[1]
    in_specs, out_specs, out_shapes, operands = [], [], [], []
    for w, layer in weights:
        _, r, c = w.shape
        n_blk = max(n for n in range(1, n_steps + 1) if r % n == 0 and (r // n) % BF16_ROWS == 0)
        blk = lambda b, i, n_blk=n_blk: jnp.minimum(b * grid[1] + i, n_blk - 1)
        in_specs.append(pl.BlockSpec((1, r // n_blk, c), lambda b, i, blk=blk, layer=layer: (layer, blk(b, i), 0)))
        out_specs.append(pl.BlockSpec((r // n_blk, c), lambda b, i, blk=blk: (blk(b, i), 0)))
        out_shapes.append(jax.ShapeDtypeStruct((r, c), BF16))
        operands.append(w)
    return in_specs, out_specs, out_shapes, operands


def _with_casts(body, n_in, n_out, n_cast):
    def kern(*refs):
        ins, rest = refs[:n_in], refs[n_in:]
        cast_in, rest = rest[:n_cast], rest[n_cast:]
        outs, rest = rest[:n_out], rest[n_out:]
        cast_out, scratch = rest[:n_cast], rest[n_cast:]
        body(*ins, *outs, *scratch)
        for src, dst in zip(cast_in, cast_out):
            dst[...] = src[0].astype(BF16)
    return kern


def _ada_kernel(cond_ref, w_ref, b_ref, o_ref):
    c = cond_ref[...]
    s = c * jax.nn.sigmoid(c)
    hi = s.astype(BF16)
    lo = (s - hi.astype(F32)).astype(BF16)
    w = w_ref[0].astype(BF16)
    o_ref[0] = _dot(hi, w) + _dot(lo, w) + b_ref[0]


def _ada(cond, ada_w, ada_b, bn=1536):
    depth, d, n = ada_w.shape
    return pl.pallas_call(
        _ada_kernel,
        grid=(depth, n // bn),
        in_specs=[
            pl.BlockSpec((HALO, d), lambda l, j: (0, 0)),
            pl.BlockSpec((1, d, bn), lambda l, j: (l, 0, j)),
            pl.BlockSpec((1, 1, bn), lambda l, j: (l, 0, j)),
        ],
        out_specs=pl.BlockSpec((1, HALO, bn), lambda l, j: (l, 0, j)),
        out_shape=jax.ShapeDtypeStruct((depth, HALO, n), F32),
        compiler_params=_params("parallel", "parallel"),
        name="ada",
    )(cond, ada_w, ada_b.reshape(depth, 1, n))


def _mod_rows(mod_ref, row, first):
    d = D_MODEL
    return [mod_ref[0, pl.ds(row, 1), (first + j) * d:(first + j + 1) * d] for j in range(3)]


def _layer_spec(a, layer):
    zeros = (0,) * (a.ndim - 1)
    return pl.BlockSpec((1,) + a.shape[1:], lambda *_: (layer,) + zeros, pipeline_mode=pl.Buffered(1))


def _decay_tables(ctx_len):
    c = RET_CHUNK
    h = np.arange(RET_HEADS, dtype=np.float64)
    lg_f = np.log1p(-np.exp2(-DECAY_EXP_FWD - h))[:, None]
    lg_b = np.log1p(-np.exp2(-DECAY_EXP_BWD - h))[:, None]
    idx = np.arange(c, dtype=np.float64)
    diff = idx[:, None] - idx[None, :]
    mask = (np.where(diff >= 0, np.exp(np.maximum(diff, 0.0)[None] * lg_f[:, :, None]), 0.0)
            + np.where(diff <= 0, np.exp(np.maximum(-diff, 0.0)[None] * lg_b[:, :, None]), 0.0))
    lane = lambda a: jnp.asarray(np.repeat(a[:, :, None], LANES, axis=2), F32)
    cidx = np.arange(ctx_len, dtype=np.float64)
    return dict(
        mask=jnp.asarray(mask, F32),
        qdec_f=lane(np.exp((idx[None] + 1.0) * lg_f)),
        kdec_f=lane(np.exp((c - 1.0 - idx[None]) * lg_f)),
        qdec_b=lane(np.exp((c - idx[None]) * lg_b)),
        kdec_b=lane(np.exp(idx[None] * lg_b)),
        ctxw_f=lane(np.exp((ctx_len - 1.0 - cidx[None]) * lg_f)),
        ctxw_b=lane(np.exp(cidx[None] * lg_b)),
        cdec_f=[float(v) for v in np.exp(c * lg_f[:, 0])],
        cdec_b=[float(v) for v in np.exp(c * lg_b[:, 0])],
    )


def _lanes(tab, n):
    return jnp.concatenate([tab] * (n // LANES), axis=1)


def _head(ref, rows, h, width):
    return ref[0, rows, h * width:(h + 1) * width]


def _scaled_bf16(k, tab):
    return (k.astype(F32) * _lanes(tab, k.shape[1])).astype(BF16)


def _init_state(s_ref, kc_ref, vc_ref, ctxw_ref):
    every = slice(None)
    for h in range(RET_HEADS):
        kw = _scaled_bf16(_head(kc_ref, every, h, RET_QK_DIM), ctxw_ref[h])
        s_ref[h] = _dot_tn(kw, _head(vc_ref, every, h, RET_V_DIM))


def _ctx_kv_kernel(x_ref, mod_ref, gain_ref, wk_ref, wv_ref, k_ref, v_ref):
    d = D_MODEL
    shift, scale, _ = _mod_rows(mod_ref, 2, 0)
    ax = _norm_mod(x_ref[0], gain_ref[0], shift, scale).astype(BF16)
    k_ref[0] = (_dot(ax, wk_ref[...]) * RET_QK_DIM ** -0.5).astype(BF16)
    for j in range(2):
        v_ref[0, :, j * d:(j + 1) * d] = _dot(ax, wv_ref[:, j * d:(j + 1) * d]).astype(BF16)


def _ctx_kv(ctx, mod, gain, w_qkvg, layer):
    bsz, t, d = ctx.shape
    row = lambda b: (b, 0, 0)
    return pl.pallas_call(
        _ctx_kv_kernel,
        grid=(bsz,),
        in_specs=[
            pl.BlockSpec((1, t, d), row),
            _layer_spec(mod, layer),
            _layer_spec(gain, layer),
            pl.BlockSpec((d, d), lambda b: (0, 1), pipeline_mode=pl.Buffered(1)),
            pl.BlockSpec((d, 2 * d), lambda b: (0, 1), pipeline_mode=pl.Buffered(1)),
        ],
        out_specs=[pl.BlockSpec((1, t, d), row), pl.BlockSpec((1, t, 2 * d), row)],
        out_shape=[
            jax.ShapeDtypeStruct((bsz, t, d), BF16),
            jax.ShapeDtypeStruct((bsz, t, 2 * d), BF16),
        ],
        compiler_params=_params("parallel"),
        name="ctx_kv",
    )(ctx, mod, gain, w_qkvg, w_qkvg)


def _rope_tile(rcos_ref, rsin_ref, ccos_ref, csin_ref, tile, tm):
    n_rows = tm // GRID_W
    row0 = tile * n_rows
    def build(r_ref, c_ref):
        rpart = jnp.concatenate(
            [jnp.broadcast_to(r_ref[pl.ds(row0 + r, 1), :], (GRID_W, LANES)) for r in range(n_rows)], axis=0)
        cpart = jnp.concatenate([c_ref[...]] * n_rows, axis=0)
        return jnp.concatenate([rpart, cpart], axis=1)
    return build(rcos_ref, ccos_ref), build(rsin_ref, csin_ref)


def _rope(r, cos, sin):
    outs = []
    for blk in range(r.shape[1] // LANES):
        xb = r[:, blk * LANES:(blk + 1) * LANES]
        tb = (blk % 2) * LANES
        outs.append(xb * cos[:, tb:tb + LANES]
                    + pltpu.roll(xb, LANES // 2, axis=1) * sin[:, tb:tb + LANES])
    return jnp.concatenate(outs, axis=1)


def _qkvg_kernel(x_ref, mod_ref, gain_ref, w_ref, rcos_ref, rsin_ref, ccos_ref, csin_ref,
                 kdec_ref, ctxw_ref, kc_ref, vc_ref, q_ref, k_ref, v_ref, g_ref, sb_ref, s_ref, *, cdec):
    d = D_MODEL
    tm = x_ref.shape[1]
    step = pl.program_id(1)
    tile = pl.num_programs(1) - 1 - step

    @pl.when(step == 0)
    def _():
        _init_state(s_ref, kc_ref, vc_ref, ctxw_ref)

    shift, scale, _ = _mod_rows(mod_ref, pl.program_id(0), 0)
    ax = _norm_mod(x_ref[0], gain_ref[0], shift, scale).astype(BF16)
    cos, sin = _rope_tile(rcos_ref, rsin_ref, ccos_ref, csin_ref, tile, tm)
    kf = _rope(_dot(ax, w_ref[:, d:2 * d]), cos, sin) * RET_QK_DIM ** -0.5
    k_ref[0] = kf.astype(BF16)
    for j in range(2):
        v_ref[0, :, j * d:(j + 1) * d] = _dot(ax, w_ref[:, (2 + j) * d:(3 + j) * d]).astype(BF16)

    for c in reversed(range(tm // RET_CHUNK)):
        rows = slice(c * RET_CHUNK, (c + 1) * RET_CHUNK)
        for h in range(RET_HEADS):
            s = s_ref[h]
            sb_ref[0, c, h] = s.astype(BF16)
            kd = (kf[rows, h * RET_QK_DIM:(h + 1) * RET_QK_DIM] * _lanes(kdec_ref[h], RET_QK_DIM)).astype(BF16)
            s_ref[h] = s * cdec[h] + _dot_tn(kd, _head(v_ref, rows, h, RET_V_DIM))

    q_ref[0] = _rope(_dot(ax, w_ref[:, 0:d]), cos, sin).astype(BF16)
    for j in range(2):
        g_ref[0, :, j * d:(j + 1) * d] = _dot(ax, w_ref[:, (4 + j) * d:(5 + j) * d]).astype(BF16)


def _qkvg(x, mod, gain, w, kc, vc, tabs, rope, layer, casts, tm=512):
    bsz, t, d = x.shape
    n_t = t // tm
    grid = (bsz, n_t)
    row = lambda b, i: (b, n_t - 1 - i, 0)
    per_b = lambda b, i: (b, 0, 0)
    n_c = tm // RET_CHUNK
    sb_block = (1, n_c, RET_HEADS, RET_QK_DIM, RET_V_DIM)
    c_in, c_out, c_shapes, c_ops = _cast_plan(casts, grid)
    in_specs = [
        pl.BlockSpec((1, tm, d), row),
        _layer_spec(mod, layer),
        _layer_spec(gain, layer),
        _const_spec(w.shape),
        *[_const_spec(r.shape) for r in rope],
        _const_spec(tabs["kdec_b"].shape),
        _const_spec(tabs["ctxw_b"].shape),
        pl.BlockSpec((1,) + kc.shape[1:], per_b),
        pl.BlockSpec((1,) + vc.shape[1:], per_b),
    ]
    out_specs = [
        pl.BlockSpec((1, tm, d), row),
        pl.BlockSpec((1, tm, d), row),
        pl.BlockSpec((1, tm, 2 * d), row),
        pl.BlockSpec((1, tm, 2 * d), row),
        pl.BlockSpec(sb_block, lambda b, i: (b, n_t - 1 - i, 0, 0, 0)),
    ]
    out_shape = [
        jax.ShapeDtypeStruct((bsz, t, d), BF16),
        jax.ShapeDtypeStruct((bsz, t, d), BF16),
        jax.ShapeDtypeStruct((bsz, t, 2 * d), BF16),
        jax.ShapeDtypeStruct((bsz, t, 2 * d), BF16),
        jax.ShapeDtypeStruct((bsz, t // RET_CHUNK) + sb_block[2:], BF16),
    ]
    outs = pl.pallas_call(
        _with_casts(functools.partial(_qkvg_kernel, cdec=tabs["cdec_b"]), len(in_specs), len(out_specs), len(casts)),
        grid=grid,
        in_specs=in_specs + c_in,
        out_specs=out_specs + c_out,
        out_shape=out_shape + c_shapes,
        scratch_shapes=[pltpu.VMEM((RET_HEADS, RET_QK_DIM, RET_V_DIM), F32)],
        compiler_params=_params("arbitrary", "arbitrary"),
        name="qkvg",
    )(x, mod, gain, w, *rope, tabs["kdec_b"], tabs["ctxw_b"], kc, vc, *c_ops)
    return outs[:len(out_specs)], outs[len(out_specs):]


def _ret_kernel(q_ref, k_ref, v_ref, g_ref, sb_ref, x_ref, kc_ref, vc_ref, mask_ref, qdf_ref, kdf_ref,
                qdb_ref, ctxw_ref, mod_ref, wo_ref, o_ref, s_ref, y_ref, *, cdec):
    @pl.when(pl.program_id(1) == 0)
    def _():
        _init_state(s_ref, kc_ref, vc_ref, ctxw_ref)

    gate = _mod_rows(mod_ref, pl.program_id(0), 0)[2]
    for c in range(x_ref.shape[1] // RET_CHUNK):
        rows = slice(c * RET_CHUNK, (c + 1) * RET_CHUNK)
        for h in range(RET_HEADS):
            q = _head(q_ref, rows, h, RET_QK_DIM)
            k = _head(k_ref, rows, h, RET_QK_DIM)
            v = _head(v_ref, rows, h, RET_V_DIM)
            p = (_dot_nt(q, k) * mask_ref[h]).astype(BF16)
            s = s_ref[h]
            o = (_dot(p, v)
                 + _dot(q, s.astype(BF16)) * _lanes(qdf_ref[h], RET_V_DIM)
                 + _dot(q, sb_ref[0, c, h]) * _lanes(qdb_ref[h], RET_V_DIM))
            o = o * lax.rsqrt(jnp.mean(o * o, axis=-1, keepdims=True) + EPS)
            g = _head(g_ref, rows, h, RET_V_DIM).astype(F32)
            y_ref[rows, h * RET_V_DIM:(h + 1) * RET_V_DIM] = (g * jax.nn.sigmoid(g) * o).astype(BF16)
            s_ref[h] = s * cdec[h] + _dot_tn(_scaled_bf16(k, kdf_ref[h]), v)
        o_ref[0, rows] = x_ref[0, rows] + gate * _dot(y_ref[rows], wo_ref[...])


def _retention(x, q, k, v, g, sb, kc, vc, mod, w_o, tabs, layer, blk=512):
    bsz, t, d = x.shape
    row = lambda b, i: (b, i, 0)
    per_b = lambda b, i: (b, 0, 0)
    sb_block = (1, blk // RET_CHUNK) + sb.shape[2:]
    consts = [tabs["mask"], tabs["qdec_f"], tabs["kdec_f"], tabs["qdec_b"], tabs["ctxw_f"]]
    return pl.pallas_call(
        functools.partial(_ret_kernel, cdec=tabs["cdec_f"]),
        grid=(bsz, t // blk),
        in_specs=[
            pl.BlockSpec((1, blk, d), row),
            pl.BlockSpec((1, blk, d), row),
            pl.BlockSpec((1, blk, 2 * d), row),
            pl.BlockSpec((1, blk, 2 * d), row),
            pl.BlockSpec(sb_block, lambda b, i: (b, i, 0, 0, 0)),
            pl.BlockSpec((1, blk, d), row),
            pl.BlockSpec((1,) + kc.shape[1:], per_b),
            pl.BlockSpec((1,) + vc.shape[1:], per_b),
            *[_const_spec(a.shape) for a in consts],
            _layer_spec(mod, layer),
            _const_spec(w_o.shape),
        ],
        out_specs=pl.BlockSpec((1, blk, d), row),
        out_shape=jax.ShapeDtypeStruct((bsz, t, d), F32),
        scratch_shapes=[pltpu.VMEM((RET_HEADS, RET_QK_DIM, RET_V_DIM), F32), pltpu.VMEM((blk, 2 * d), BF16)],
        compiler_params=_params("parallel", "arbitrary"),
        name="ret",
    )(q, k, v, g, sb, x, kc, vc, *consts, mod, w_o)


def _ffn_chunks(f):
    step = 768
    return [(c, min(c + step, f)) for c in range(0, f, step)]


def _ffn_kernel(x_ref, mod_ref, gain_ref, w1_ref, w3_ref, w2_ref, fgain_ref, o_ref, *, final_norm, sub):
    shift, scale, gate = _mod_rows(mod_ref, pl.program_id(0), 3)
    for r0 in range(0, x_ref.shape[1], sub):
        rows = slice(r0, r0 + sub)
        x = x_ref[0, rows]
        fx = _norm_mod(x, gain_ref[0], shift, scale).astype(BF16)
        acc = None
        for c0, c1 in _ffn_chunks(w1_ref.shape[1]):
            a = _dot(fx, w1_ref[:, c0:c1])
            hid = (a * jax.nn.sigmoid(a) * _dot(fx, w3_ref[:, c0:c1])).astype(BF16)
            part = _dot(hid, w2_ref[c0:c1, :])
            acc = part if acc is None else acc + part
        out = x + gate * acc
        if final_norm:
            out = out * lax.rsqrt(jnp.mean(out * out, axis=-1, keepdims=True) + EPS) * fgain_ref[...]
        o_ref[0, rows] = out


def _ffn(x, mod, gain, w1, w3, w2, fgain, layer, final_norm, casts=(), tm=1024, sub=512):
    bsz, t, d = x.shape
    grid = (bsz, t // tm)
    row = lambda b, i: (b, i, 0)
    c_in, c_out, c_shapes, c_ops = _cast_plan(casts, grid)
    in_specs = [
        pl.BlockSpec((1, tm, d), row),
        _layer_spec(mod, layer),
        _layer_spec(gain, layer),
        _const_spec(w1.shape),
        _const_spec(w3.shape),
        _const_spec(w2.shape),
        _const_spec(fgain.shape),
    ]
    outs = pl.pallas_call(
        _with_casts(functools.partial(_ffn_kernel, final_norm=final_norm, sub=sub), len(in_specs), 1, len(casts)),
        grid=grid,
        in_specs=in_specs + c_in,
        out_specs=[pl.BlockSpec((1, tm, d), row)] + c_out,
        out_shape=[jax.ShapeDtypeStruct((bsz, t, d), F32)] + c_shapes,
        compiler_params=_params("arbitrary", "arbitrary"),
        name="ffn_final" if final_norm else "ffn",
    )(x, mod, gain, w1, w3, w2, fgain, *c_ops)
    return outs[0], outs[1:]


def _conv_kernel(x_ref, prev_ref, next_ref, mod_ref, gain_ref, win_ref, cw_ref, wout_ref, o_ref, u_ref):
    d = D_MODEL
    tm = x_ref.shape[1]
    i, last = pl.program_id(1), pl.num_programs(1) - 1
    shift, scale, gate = _mod_rows(mod_ref, pl.program_id(0), 0)
    x = x_ref[0]
    halo = jnp.concatenate([prev_ref[0], next_ref[0]], axis=0)
    ax = jnp.concatenate([_norm_mod(x, gain_ref[0], shift, scale),
                          _norm_mod(halo, gain_ref[0], shift, scale)], axis=0).astype(BF16)
    u = _dot(ax, win_ref[:, d:2 * d]) * _dot(ax, win_ref[:, 2 * d:3 * d])
    u_ref[0:HALO] = u[tm:tm + HALO] * jnp.where(i == 0, 0.0, 1.0)
    u_ref[HALO:HALO + tm] = u[0:tm]
    u_ref[HALO + tm:] = u[tm + HALO:] * jnp.where(i == last, 0.0, 1.0)
    cw = cw_ref[0]
    conv = (u_ref[pl.ds(HALO - 1, tm)] * cw[0:1] + u[0:tm] * cw[1:2]
            + u_ref[pl.ds(HALO + 1, tm)] * cw[2:3])
    b_gate = _dot(ax[0:tm], win_ref[:, 0:d])
    o_ref[0] = x + gate * _dot((b_gate * conv).astype(BF16), wout_ref[...])


def _conv(x, mod, gain, w_in, conv_w, w_out, layer, tm=1024):
    bsz, t, d = x.shape
    per = tm // HALO
    n_halo = t // HALO
    row = lambda b, i: (b, i, 0)
    return pl.pallas_call(
        _conv_kernel,
        grid=(bsz, t // tm),
        in_specs=[
            pl.BlockSpec((1, tm, d), row),
            pl.BlockSpec((1, HALO, d), lambda b, i: (b, jnp.maximum(i * per - 1, 0), 0)),
            pl.BlockSpec((1, HALO, d), lambda b, i: (b, jnp.minimum((i + 1) * per, n_halo - 1), 0)),
            _layer_spec(mod, layer),
            _layer_spec(gain, layer),
            _const_spec(w_in.shape),
            _layer_spec(conv_w, 0),
            _const_spec(w_out.shape),
        ],
        out_specs=pl.BlockSpec((1, tm, d), row),
        out_shape=jax.ShapeDtypeStruct((bsz, t, d), F32),
        scratch_shapes=[pltpu.VMEM((tm + 2 * HALO, d), F32)],
        compiler_params=_params("parallel", "parallel"),
        name="conv",
    )(x, x, x, mod, gain, w_in, conv_w, w_out)


def _rope_tables(t):
    n = LANES // 2
    freqs = ROPE_BASE ** (-np.arange(n, dtype=np.float64) / n)
    def tabs(count):
        ang = np.arange(count, dtype=np.float64)[:, None] * freqs[None, :]
        return (jnp.asarray(np.concatenate([np.cos(ang), np.cos(ang)], axis=1), F32),
                jnp.asarray(np.concatenate([-np.sin(ang), np.sin(ang)], axis=1), F32))
    rcos, rsin = tabs(t // GRID_W)
    ccos, csin = tabs(GRID_W)
    return rcos, rsin, ccos, csin


def kernel(x, c, ctx, c_ctx, ada_w, ada_b, norm_mix, norm_ffn, ret_w_qkvg, ret_w_o, conv_w_in, conv_w,
           conv_w_out, ffn_w1, ffn_w3, ffn_w2, final_norm):
    bsz, t, d = x.shape
    cond = jnp.concatenate([c, c_ctx[None], jnp.zeros((HALO - bsz - 1, d), F32)], axis=0)
    mod = _ada(cond, ada_w, ada_b)
    depth = mod.shape[0]
    gain_mix = norm_mix.reshape(depth, 1, d)
    gain_ffn = norm_ffn.reshape(depth, 1, d)
    fgain = final_norm.reshape(1, d)
    tabs = _decay_tables(ctx.shape[1])

    w_qkvg = ret_w_qkvg[0].astype(BF16)
    kc, vc = _ctx_kv(ctx, mod, gain_mix, w_qkvg, 0)
    (q, k, v, g, sb), (w_o, w1, w3, w2) = _qkvg(
        x, mod, gain_mix, w_qkvg, kc, vc, tabs, _rope_tables(t), 0,
        casts=[(ret_w_o, 0), (ffn_w1, 0), (ffn_w3, 0), (ffn_w2, 0)])
    hx = _retention(x, q, k, v, g, sb, kc, vc, mod, w_o, tabs, 0)
    hx, (w_in, w_out, w1, w3, w2) = _ffn(
        hx, mod, gain_ffn, w1, w3, w2, fgain, 0, final_norm=False,
        casts=[(conv_w_in, 0), (conv_w_out, 0), (ffn_w1, 1), (ffn_w3, 1), (ffn_w2, 1)])

    hx = _conv(hx, mod, gain_mix, w_in, conv_w, w_out, 1)
    return _ffn(hx, mod, gain_ffn, w1, w3, w2, fgain, 1, final_norm=True, sub=256)[0]
```

```python
import functools

import jax
import jax.numpy as jnp
import numpy as np
from jax import lax
from jax.experimental import pallas as pl
from jax.experimental.pallas import tpu as pltpu

D_MODEL = 1024
GRID_W = 64
RET_HEADS = 4
RET_QK_DIM = D_MODEL // RET_HEADS
RET_V_DIM = 2 * D_MODEL // RET_HEADS
DECAY_EXP_FWD = 5.0
DECAY_EXP_BWD = 5.5
ROPE_BASE = 10000.0
EPS = 1e-6

RET_CHUNK = 256
LANES = 128
HALO = 8
VMEM_LIMIT = 56 * 1024 * 1024

BF16 = jnp.bfloat16
F32 = jnp.float32


def _const_spec(shape):
    return pl.BlockSpec(shape, lambda *_: (0,) * len(shape), pipeline_mode=pl.Buffered(1))


def _params(*sem, **kw):
    return pltpu.CompilerParams(dimension_semantics=sem, vmem_limit_bytes=VMEM_LIMIT, **kw)


def _norm_mod(x, gain, shift, scale):
    y = x * lax.rsqrt(jnp.mean(x * x, axis=-1, keepdims=True) + EPS) * gain
    return y * (1.0 + scale) + shift


def _dot(a, b):
    return jnp.dot(a, b, preferred_element_type=F32)


def _dot_nt(a, b):
    return lax.dot_general(a, b, (((1,), (1,)), ((), ())), preferred_element_type=F32)


def _dot_tn(a, b):
    return lax.dot_general(a, b, (((0,), (0,)), ((), ())), preferred_element_type=F32)


BF16_ROWS = 16


def _cast_plan(weights, grid):
    n_steps = grid[0] * grid[1]
    in_specs, out_specs, out_shapes, operands = [], [], [], []
    for w, layer in weights:
        _, r, c = w.shape
        n_blk = max(n for n in range(1, n_steps + 1) if r % n == 0 and (r // n) % BF16_ROWS == 0)
        blk = lambda b, i, n_blk=n_blk: jnp.minimum(b * grid[1] + i, n_blk - 1)
        in_specs.append(pl.BlockSpec((1, r // n_blk, c), lambda b, i, blk=blk, layer=layer: (layer, blk(b, i), 0)))
        out_specs.append(pl.BlockSpec((r // n_blk, c), lambda b, i, blk=blk: (blk(b, i), 0)))
        out_shapes.append(jax.ShapeDtypeStruct((r, c), BF16))
        operands.append(w)
    return in_specs, out_specs, out_shapes, operands


def _with_casts(body, n_in, n_out, n_cast):
    def kern(*refs):
        ins, rest = refs[:n_in], refs[n_in:]
        cast_in, rest = rest[:n_cast], rest[n_cast:]
        outs, rest = rest[:n_out], rest[n_out:]
        cast_out, scratch = rest[:n_cast], rest[n_cast:]
        body(*ins, *outs, *scratch)
        for src, dst in zip(cast_in, cast_out):
            dst[...] = src[0].astype(BF16)
    return kern


def _ada_kernel(cond_ref, w_ref, b_ref, o_ref):
    c = cond_ref[...]
    s = c * jax.nn.sigmoid(c)
    hi = s.astype(BF16)
    lo = (s - hi.astype(F32)).astype(BF16)
    w = w_ref[0].astype(BF16)
    o_ref[0] = _dot(hi, w) + _dot(lo, w) + b_ref[0]


def _ada(cond, ada_w, ada_b, bn=3072):
    depth, d, n = ada_w.shape
    return pl.pallas_call(
        _ada_kernel,
        grid=(depth, n // bn),
        in_specs=[
            pl.BlockSpec((HALO, d), lambda l, j: (0, 0)),
            pl.BlockSpec((1, d, bn), lambda l, j: (l, 0, j)),
            pl.BlockSpec((1, 1, bn), lambda l, j: (l, 0, j)),
        ],
        out_specs=pl.BlockSpec((1, HALO, bn), lambda l, j: (l, 0, j)),
        out_shape=jax.ShapeDtypeStruct((depth, HALO, n), F32),
        compiler_params=_params("parallel", "parallel"),
        name="ada",
    )(cond, ada_w, ada_b.reshape(depth, 1, n))


def _mod_rows(mod_ref, row, first):
    d = D_MODEL
    return [mod_ref[0, pl.ds(row, 1), (first + j) * d:(first + j + 1) * d] for j in range(3)]


def _layer_spec(a, layer):
    zeros = (0,) * (a.ndim - 1)
    return pl.BlockSpec((1,) + a.shape[1:], lambda *_: (layer,) + zeros, pipeline_mode=pl.Buffered(1))


def _decay_tables(ctx_len):
    c = RET_CHUNK
    h = np.arange(RET_HEADS, dtype=np.float64)
    lg_f = np.log1p(-np.exp2(-DECAY_EXP_FWD - h))[:, None]
    lg_b = np.log1p(-np.exp2(-DECAY_EXP_BWD - h))[:, None]
    idx = np.arange(c, dtype=np.float64)
    diff = idx[:, None] - idx[None, :]
    mask = (np.where(diff >= 0, np.exp(np.maximum(diff, 0.0)[None] * lg_f[:, :, None]), 0.0)
            + np.where(diff <= 0, np.exp(np.maximum(-diff, 0.0)[None] * lg_b[:, :, None]), 0.0))
    lane = lambda a: jnp.asarray(np.repeat(a[:, :, None], LANES, axis=2), F32)
    cidx = np.arange(ctx_len, dtype=np.float64)
    return dict(
        mask=jnp.asarray(mask, F32),
        qdec_f=lane(np.exp((idx[None] + 1.0) * lg_f)),
        kdec_f=lane(np.exp((c - 1.0 - idx[None]) * lg_f)),
        qdec_b=lane(np.exp((c - idx[None]) * lg_b)),
        kdec_b=lane(np.exp(idx[None] * lg_b)),
        ctxw_f=lane(np.exp((ctx_len - 1.0 - cidx[None]) * lg_f)),
        ctxw_b=lane(np.exp(cidx[None] * lg_b)),
        cdec_f=[float(v) for v in np.exp(c * lg_f[:, 0])],
        cdec_b=[float(v) for v in np.exp(c * lg_b[:, 0])],
    )


def _lanes(tab, n):
    return jnp.concatenate([tab] * (n // LANES), axis=1)


def _head(ref, rows, h, width):
    return ref[0, rows, h * width:(h + 1) * width]


def _scaled_bf16(k, tab):
    return (k.astype(F32) * _lanes(tab, k.shape[1])).astype(BF16)


def _init_state(s_ref, kc_ref, vc_ref, ctxw_ref):
    every = slice(None)
    for h in range(RET_HEADS):
        kw = _scaled_bf16(_head(kc_ref, every, h, RET_QK_DIM), ctxw_ref[h])
        s_ref[h] = _dot_tn(kw, _head(vc_ref, every, h, RET_V_DIM))


def _ctx_kv_kernel(x_ref, mod_ref, gain_ref, wk_ref, wv_ref, k_ref, v_ref):
    d = D_MODEL
    shift, scale, _ = _mod_rows(mod_ref, 2, 0)
    ax = _norm_mod(x_ref[0], gain_ref[0], shift, scale).astype(BF16)
    k_ref[0] = (_dot(ax, wk_ref[...]) * RET_QK_DIM ** -0.5).astype(BF16)
    for j in range(2):
        v_ref[0, :, j * d:(j + 1) * d] = _dot(ax, wv_ref[:, j * d:(j + 1) * d]).astype(BF16)


def _ctx_kv(ctx, mod, gain, w_qkvg, layer):
    bsz, t, d = ctx.shape
    row = lambda b: (b, 0, 0)
    return pl.pallas_call(
        _ctx_kv_kernel,
        grid=(bsz,),
        in_specs=[
            pl.BlockSpec((1, t, d), row),
            _layer_spec(mod, layer),
            _layer_spec(gain, layer),
            pl.BlockSpec((d, d), lambda b: (0, 1), pipeline_mode=pl.Buffered(1)),
            pl.BlockSpec((d, 2 * d), lambda b: (0, 1), pipeline_mode=pl.Buffered(1)),
        ],
        out_specs=[pl.BlockSpec((1, t, d), row), pl.BlockSpec((1, t, 2 * d), row)],
        out_shape=[
            jax.ShapeDtypeStruct((bsz, t, d), BF16),
            jax.ShapeDtypeStruct((bsz, t, 2 * d), BF16),
        ],
        compiler_params=_params("parallel"),
        name="ctx_kv",
    )(ctx, mod, gain, w_qkvg, w_qkvg)


def _rope_tile(rcos_ref, rsin_ref, ccos_ref, csin_ref, tile, tm):
    n_rows = tm // GRID_W
    row0 = tile * n_rows
    def build(r_ref, c_ref):
        rpart = jnp.concatenate(
            [jnp.broadcast_to(r_ref[pl.ds(row0 + r, 1), :], (GRID_W, LANES)) for r in range(n_rows)], axis=0)
        cpart = jnp.concatenate([c_ref[...]] * n_rows, axis=0)
        return jnp.concatenate([rpart, cpart], axis=1)
    return build(rcos_ref, ccos_ref), build(rsin_ref, csin_ref)


def _rope(r, cos, sin):
    outs = []
    for blk in range(r.shape[1] // LANES):
        xb = r[:, blk * LANES:(blk + 1) * LANES]
        tb = (blk % 2) * LANES
        outs.append(xb * cos[:, tb:tb + LANES]
                    + pltpu.roll(xb, LANES // 2, axis=1) * sin[:, tb:tb + LANES])
    return jnp.concatenate(outs, axis=1)


def _qkvg_kernel(x_ref, mod_ref, gain_ref, w_ref, rcos_ref, rsin_ref, ccos_ref, csin_ref,
                 kdec_ref, ctxw_ref, kc_ref, vc_ref, q_ref, k_ref, v_ref, g_ref, sb_ref, s_ref, *, cdec):
    d = D_MODEL
    tm = x_ref.shape[1]
    step = pl.program_id(1)
    tile = pl.num_programs(1) - 1 - step

    @pl.when(step == 0)
    def _():
        _init_state(s_ref, kc_ref, vc_ref, ctxw_ref)

    shift, scale, _ = _mod_rows(mod_ref, pl.program_id(0), 0)
    ax = _norm_mod(x_ref[0], gain_ref[0], shift, scale).astype(BF16)
    cos, sin = _rope_tile(rcos_ref, rsin_ref, ccos_ref, csin_ref, tile, tm)
    kf = _rope(_dot(ax, w_ref[:, d:2 * d]), cos, sin) * RET_QK_DIM ** -0.5
    k_ref[0] = kf.astype(BF16)
    for j in range(2):
        v_ref[0, :, j * d:(j + 1) * d] = _dot(ax, w_ref[:, (2 + j) * d:(3 + j) * d]).astype(BF16)

    for c in reversed(range(tm // RET_CHUNK)):
        rows = slice(c * RET_CHUNK, (c + 1) * RET_CHUNK)
        for h in range(RET_HEADS):
            s = s_ref[h]
            sb_ref[0, c, h] = s.astype(BF16)
            kd = (kf[rows, h * RET_QK_DIM:(h + 1) * RET_QK_DIM] * _lanes(kdec_ref[h], RET_QK_DIM)).astype(BF16)
            s_ref[h] = s * cdec[h] + _dot_tn(kd, _head(v_ref, rows, h, RET_V_DIM))

    q_ref[0] = _rope(_dot(ax, w_ref[:, 0:d]), cos, sin).astype(BF16)
    for j in range(2):
        g_ref[0, :, j * d:(j + 1) * d] = _dot(ax, w_ref[:, (4 + j) * d:(5 + j) * d]).astype(BF16)


def _qkvg(x, mod, gain, w, kc, vc, tabs, rope, layer, casts, tm=512):
    bsz, t, d = x.shape
    n_t = t // tm
    grid = (bsz, n_t)
    row = lambda b, i: (b, n_t - 1 - i, 0)
    per_b = lambda b, i: (b, 0, 0)
    n_c = tm // RET_CHUNK
    sb_block = (1, n_c, RET_HEADS, RET_QK_DIM, RET_V_DIM)
    c_in, c_out, c_shapes, c_ops = _cast_plan(casts, grid)
    in_specs = [
        pl.BlockSpec((1, tm, d), row),
        _layer_spec(mod, layer),
        _layer_spec(gain, layer),
        _const_spec(w.shape),
        *[_const_spec(r.shape) for r in rope],
        _const_spec(tabs["kdec_b"].shape),
        _const_spec(tabs["ctxw_b"].shape),
        pl.BlockSpec((1,) + kc.shape[1:], per_b),
        pl.BlockSpec((1,) + vc.shape[1:], per_b),
    ]
    out_specs = [
        pl.BlockSpec((1, tm, d), row),
        pl.BlockSpec((1, tm, d), row),
        pl.BlockSpec((1, tm, 2 * d), row),
        pl.BlockSpec((1, tm, 2 * d), row),
        pl.BlockSpec(sb_block, lambda b, i: (b, n_t - 1 - i, 0, 0, 0)),
    ]
    out_shape = [
        jax.ShapeDtypeStruct((bsz, t, d), BF16),
        jax.ShapeDtypeStruct((bsz, t, d), BF16),
        jax.ShapeDtypeStruct((bsz, t, 2 * d), BF16),
        jax.ShapeDtypeStruct((bsz, t, 2 * d), BF16),
        jax.ShapeDtypeStruct((bsz, t // RET_CHUNK) + sb_block[2:], BF16),
    ]
    outs = pl.pallas_call(
        _with_casts(functools.partial(_qkvg_kernel, cdec=tabs["cdec_b"]), len(in_specs), len(out_specs), len(casts)),
        grid=grid,
        in_specs=in_specs + c_in,
        out_specs=out_specs + c_out,
        out_shape=out_shape + c_shapes,
        scratch_shapes=[pltpu.VMEM((RET_HEADS, RET_QK_DIM, RET_V_DIM), F32)],
        compiler_params=_params("arbitrary", "arbitrary"),
        name="qkvg",
    )(x, mod, gain, w, *rope, tabs["kdec_b"], tabs["ctxw_b"], kc, vc, *c_ops)
    return outs[:len(out_specs)], outs[len(out_specs):]


def _ret_kernel(q_ref, k_ref, v_ref, g_ref, sb_ref, x_ref, kc_ref, vc_ref, mask_ref, qdf_ref, kdf_ref,
                qdb_ref, ctxw_ref, mod_ref, wo_ref, o_ref, s_ref, y_ref, *, cdec):
    @pl.when(pl.program_id(1) == 0)
    def _():
        _init_state(s_ref, kc_ref, vc_ref, ctxw_ref)

    gate = _mod_rows(mod_ref, pl.program_id(0), 0)[2]
    for c in range(x_ref.shape[1] // RET_CHUNK):
        rows = slice(c * RET_CHUNK, (c + 1) * RET_CHUNK)
        for h in range(RET_HEADS):
            q = _head(q_ref, rows, h, RET_QK_DIM)
            k = _head(k_ref, rows, h, RET_QK_DIM)
            v = _head(v_ref, rows, h, RET_V_DIM)
            p = (_dot_nt(q, k) * mask_ref[h]).astype(BF16)
            s = s_ref[h]
            qf32 = q.astype(F32)
            qf = (qf32 * _lanes(qdf_ref[h], RET_QK_DIM)).astype(BF16)
            qb = (qf32 * _lanes(qdb_ref[h], RET_QK_DIM)).astype(BF16)
            o = _dot(p, v) + _dot(qf, s.astype(BF16)) + _dot(qb, sb_ref[0, c, h])
            o = o * lax.rsqrt(jnp.mean(o * o, axis=-1, keepdims=True) + EPS)
            g = _head(g_ref, rows, h, RET_V_DIM).astype(F32)
            y_ref[rows, h * RET_V_DIM:(h + 1) * RET_V_DIM] = (g * jax.nn.sigmoid(g) * o).astype(BF16)
            s_ref[h] = s * cdec[h] + _dot_tn(_scaled_bf16(k, kdf_ref[h]), v)
    o_ref[0] = x_ref[0] + gate * _dot(y_ref[...], wo_ref[...])


def _retention(x, q, k, v, g, sb, kc, vc, mod, w_o, tabs, layer, blk=512):
    bsz, t, d = x.shape
    row = lambda b, i: (b, i, 0)
    per_b = lambda b, i: (b, 0, 0)
    sb_block = (1, blk // RET_CHUNK) + sb.shape[2:]
    consts = [tabs["mask"], tabs["qdec_f"], tabs["kdec_f"], tabs["qdec_b"], tabs["ctxw_f"]]
    return pl.pallas_call(
        functools.partial(_ret_kernel, cdec=tabs["cdec_f"]),
        grid=(bsz, t // blk),
        in_specs=[
            pl.BlockSpec((1, blk, d), row),
            pl.BlockSpec((1, blk, d), row),
            pl.BlockSpec((1, blk, 2 * d), row),
            pl.BlockSpec((1, blk, 2 * d), row),
            pl.BlockSpec(sb_block, lambda b, i: (b, i, 0, 0, 0)),
            pl.BlockSpec((1, blk, d), row),
            pl.BlockSpec((1,) + kc.shape[1:], per_b),
            pl.BlockSpec((1,) + vc.shape[1:], per_b),
            *[_const_spec(a.shape) for a in consts],
            _layer_spec(mod, layer),
            _const_spec(w_o.shape),
        ],
        out_specs=pl.BlockSpec((1, blk, d), row),
        out_shape=jax.ShapeDtypeStruct((bsz, t, d), F32),
        scratch_shapes=[pltpu.VMEM((RET_HEADS, RET_QK_DIM, RET_V_DIM), F32), pltpu.VMEM((blk, 2 * d), BF16)],
        compiler_params=_params("parallel", "arbitrary"),
        name="ret",
    )(q, k, v, g, sb, x, kc, vc, *consts, mod, w_o)


def _ffn_chunks(f):
    step = 768
    return [(c, min(c + step, f)) for c in range(0, f, step)]


def _ffn_kernel(x_ref, mod_ref, gain_ref, w1_ref, w3_ref, w2_ref, fgain_ref, o_ref, *, final_norm, sub):
    shift, scale, gate = _mod_rows(mod_ref, pl.program_id(0), 3)
    for r0 in range(0, x_ref.shape[1], sub):
        rows = slice(r0, r0 + sub)
        x = x_ref[0, rows]
        fx = _norm_mod(x, gain_ref[0], shift, scale).astype(BF16)
        acc = None
        for c0, c1 in _ffn_chunks(w1_ref.shape[1]):
            a = _dot(fx, w1_ref[:, c0:c1])
            hid = (a * jax.nn.sigmoid(a) * _dot(fx, w3_ref[:, c0:c1])).astype(BF16)
            part = _dot(hid, w2_ref[c0:c1, :])
            acc = part if acc is None else acc + part
        out = x + gate * acc
        if final_norm:
            out = out * lax.rsqrt(jnp.mean(out * out, axis=-1, keepdims=True) + EPS) * fgain_ref[...]
        o_ref[0, rows] = out


def _ffn(x, mod, gain, w1, w3, w2, fgain, layer, final_norm, casts=(), tm=1024, sub=512):
    bsz, t, d = x.shape
    grid = (bsz, t // tm)
    row = lambda b, i: (b, i, 0)
    c_in, c_out, c_shapes, c_ops = _cast_plan(casts, grid)
    in_specs = [
        pl.BlockSpec((1, tm, d), row),
        _layer_spec(mod, layer),
        _layer_spec(gain, layer),
        _const_spec(w1.shape),
        _const_spec(w3.shape),
        _const_spec(w2.shape),
        _const_spec(fgain.shape),
    ]
    outs = pl.pallas_call(
        _with_casts(functools.partial(_ffn_kernel, final_norm=final_norm, sub=sub), len(in_specs), 1, len(casts)),
        grid=grid,
        in_specs=in_specs + c_in,
        out_specs=[pl.BlockSpec((1, tm, d), row)] + c_out,
        out_shape=[jax.ShapeDtypeStruct((bsz, t, d), F32)] + c_shapes,
        compiler_params=_params("arbitrary", "arbitrary"),
        name="ffn_final" if final_norm else "ffn",
    )(x, mod, gain, w1, w3, w2, fgain, *c_ops)
    return outs[0], outs[1:]


def _conv_kernel(x_ref, prev_ref, next_ref, mod_ref, gain_ref, win_ref, cw_ref, wout_ref, o_ref, u_ref):
    d = D_MODEL
    tm = x_ref.shape[1]
    i, last = pl.program_id(1), pl.num_programs(1) - 1
    shift, scale, gate = _mod_rows(mod_ref, pl.program_id(0), 0)
    x = x_ref[0]
    halo = jnp.concatenate([prev_ref[0], next_ref[0]], axis=0)
    ax = jnp.concatenate([_norm_mod(x, gain_ref[0], shift, scale),
                          _norm_mod(halo, gain_ref[0], shift, scale)], axis=0).astype(BF16)
    u = _dot(ax, win_ref[:, d:2 * d]) * _dot(ax, win_ref[:, 2 * d:3 * d])
    u_ref[0:HALO] = u[tm:tm + HALO] * jnp.where(i == 0, 0.0, 1.0)
    u_ref[HALO:HALO + tm] = u[0:tm]
    u_ref[HALO + tm:] = u[tm + HALO:] * jnp.where(i == last, 0.0, 1.0)
    cw = cw_ref[0]
    conv = (u_ref[pl.ds(HALO - 1, tm)] * cw[0:1] + u[0:tm] * cw[1:2]
            + u_ref[pl.ds(HALO + 1, tm)] * cw[2:3])
    b_gate = _dot(ax[0:tm], win_ref[:, 0:d])
    o_ref[0] = x + gate * _dot((b_gate * conv).astype(BF16), wout_ref[...])


def _conv(x, mod, gain, w_in, conv_w, w_out, layer, tm=1024):
    bsz, t, d = x.shape
    per = tm // HALO
    n_halo = t // HALO
    row = lambda b, i: (b, i, 0)
    return pl.pallas_call(
        _conv_kernel,
        grid=(bsz, t // tm),
        in_specs=[
            pl.BlockSpec((1, tm, d), row),
            pl.BlockSpec((1, HALO, d), lambda b, i: (b, jnp.maximum(i * per - 1, 0), 0)),
            pl.BlockSpec((1, HALO, d), lambda b, i: (b, jnp.minimum((i + 1) * per, n_halo - 1), 0)),
            _layer_spec(mod, layer),
            _layer_spec(gain, layer),
            _const_spec(w_in.shape),
            _layer_spec(conv_w, 0),
            _const_spec(w_out.shape),
        ],
        out_specs=pl.BlockSpec((1, tm, d), row),
        out_shape=jax.ShapeDtypeStruct((bsz, t, d), F32),
        scratch_shapes=[pltpu.VMEM((tm + 2 * HALO, d), F32)],
        compiler_params=_params("parallel", "parallel"),
        name="conv",
    )(x, x, x, mod, gain, w_in, conv_w, w_out)


def _rope_tables(t):
    n = LANES // 2
    freqs = ROPE_BASE ** (-np.arange(n, dtype=np.float64) / n)
    def tabs(count):
        ang = np.arange(count, dtype=np.float64)[:, None] * freqs[None, :]
        return (jnp.asarray(np.concatenate([np.cos(ang), np.cos(ang)], axis=1), F32),
                jnp.asarray(np.concatenate([-np.sin(ang), np.sin(ang)], axis=1), F32))
    rcos, rsin = tabs(t // GRID_W)
    ccos, csin = tabs(GRID_W)
    return rcos, rsin, ccos, csin


def kernel(x, c, ctx, c_ctx, ada_w, ada_b, norm_mix, norm_ffn, ret_w_qkvg, ret_w_o, conv_w_in, conv_w,
           conv_w_out, ffn_w1, ffn_w3, ffn_w2, final_norm):
    bsz, t, d = x.shape
    cond = jnp.concatenate([c, c_ctx[None], jnp.zeros((HALO - bsz - 1, d), F32)], axis=0)
    mod = _ada(cond, ada_w, ada_b)
    depth = mod.shape[0]
    gain_mix = norm_mix.reshape(depth, 1, d)
    gain_ffn = norm_ffn.reshape(depth, 1, d)
    fgain = final_norm.reshape(1, d)
    tabs = _decay_tables(ctx.shape[1])

    w_qkvg = ret_w_qkvg[0].astype(BF16)
    kc, vc = _ctx_kv(ctx, mod, gain_mix, w_qkvg, 0)
    (q, k, v, g, sb), (w_o, w1, w3, w2) = _qkvg(
        x, mod, gain_mix, w_qkvg, kc, vc, tabs, _rope_tables(t), 0,
        casts=[(ret_w_o, 0), (ffn_w1, 0), (ffn_w3, 0), (ffn_w2, 0)])
    hx = _retention(x, q, k, v, g, sb, kc, vc, mod, w_o, tabs, 0)
    hx, (w_in, w_out, w1, w3, w2) = _ffn(
        hx, mod, gain_ffn, w1, w3, w2, fgain, 0, final_norm=False,
        casts=[(conv_w_in, 0), (conv_w_out, 0), (ffn_w1, 1), (ffn_w3, 1), (ffn_w2, 1)])

    hx = _conv(hx, mod, gain_mix, w_in, conv_w, w_out, 1)
    return _ffn(hx, mod, gain_ffn, w1, w3, w2, fgain, 1, final_norm=True)[0]
```

```python
import functools

import jax
import jax.numpy as jnp
import numpy as np
from jax import lax
from jax.experimental import pallas as pl
from jax.experimental.pallas import tpu as pltpu

D_MODEL = 1024
GRID_W = 64
RET_HEADS = 4
RET_QK_DIM = D_MODEL // RET_HEADS
RET_V_DIM = 2 * D_MODEL // RET_HEADS
DECAY_EXP_FWD = 5.0
DECAY_EXP_BWD = 5.5
ROPE_BASE = 10000.0
EPS = 1e-6

RET_CHUNK = 256
LANES = 128
HALO = 8
VMEM_LIMIT = 56 * 1024 * 1024

BF16 = jnp.bfloat16
F32 = jnp.float32


def _const_spec(shape):
    return pl.BlockSpec(shape, lambda *_: (0,) * len(shape), pipeline_mode=pl.Buffered(1))


def _params(*sem):
    return pltpu.CompilerParams(dimension_semantics=sem, vmem_limit_bytes=VMEM_LIMIT)


def _norm_mod(x, gain, shift, scale):
    y = x * lax.rsqrt(jnp.mean(x * x, axis=-1, keepdims=True) + EPS) * gain
    return y * (1.0 + scale) + shift


def _dot(a, b):
    return jnp.dot(a, b, preferred_element_type=F32)


def _dot_nt(a, b):
    return lax.dot_general(a, b, (((1,), (1,)), ((), ())), preferred_element_type=F32)


def _dot_tn(a, b):
    return lax.dot_general(a, b, (((0,), (0,)), ((), ())), preferred_element_type=F32)


BF16_ROWS = 16


def _cast_plan(weights, grid):
    n_steps = grid[0] * grid[1]
    in_specs, out_specs, out_shapes, operands = [], [], [], []
    for w, layer in weights:
        _, r, c = w.shape
        n_blk = max(n for n in range(1, n_steps + 1) if r % n == 0 and (r // n) % BF16_ROWS == 0)
        blk = lambda b, i, n_blk=n_blk: jnp.minimum(b * grid[1] + i, n_blk - 1)
        in_specs.append(pl.BlockSpec((1, r // n_blk, c), lambda b, i, blk=blk, layer=layer: (layer, blk(b, i), 0)))
        out_specs.append(pl.BlockSpec((r // n_blk, c), lambda b, i, blk=blk: (blk(b, i), 0)))
        out_shapes.append(jax.ShapeDtypeStruct((r, c), BF16))
        operands.append(w)
    return in_specs, out_specs, out_shapes, operands


def _with_casts(body, n_in, n_out, n_cast):
    def kern(*refs):
        ins, rest = refs[:n_in], refs[n_in:]
        cast_in, rest = rest[:n_cast], rest[n_cast:]
        outs, rest = rest[:n_out], rest[n_out:]
        cast_out, scratch = rest[:n_cast], rest[n_cast:]
        body(*ins, *outs, *scratch)
        for src, dst in zip(cast_in, cast_out):
            dst[...] = src[0].astype(BF16)
    return kern


COND_ROWS = HALO
ADA_FIRST = 2 * D_MODEL
ADA_BLOCK = 384


def _ada_rows(cond, w, b):
    s = cond * jax.nn.sigmoid(cond)
    hi = s.astype(BF16)
    lo = (s - hi.astype(F32)).astype(BF16)
    upper = lax.broadcasted_iota(jnp.int32, s.shape, 0) < COND_ROWS // 2
    out = _dot(jnp.where(upper, hi, lo), w.astype(BF16))
    return out + pltpu.roll(out, COND_ROWS // 2, axis=0) + b


def _ada_kernel(cond_ref, w_ref, b_ref, o_ref):
    o_ref[0] = _ada_rows(cond_ref[...], w_ref[0], b_ref[0])


def _ada_first(cond, ada_w, ada_b3, layer, casts, bn=256):
    _, d, _ = ada_w.shape
    grid = (1, ADA_FIRST // bn)
    c_in, c_out, c_shapes, c_ops = _cast_plan(casts, grid)
    in_specs = [
        pl.BlockSpec((COND_ROWS, d), lambda l, j: (0, 0)),
        pl.BlockSpec((1, d, bn), lambda l, j: (layer, 0, j)),
        pl.BlockSpec((1, 1, bn), lambda l, j: (layer, 0, j)),
    ]
    outs = pl.pallas_call(
        _with_casts(_ada_kernel, len(in_specs), 1, len(casts)),
        grid=grid,
        in_specs=in_specs + c_in,
        out_specs=[pl.BlockSpec((1, COND_ROWS, bn), lambda l, j: (0, 0, j))] + c_out,
        out_shape=[jax.ShapeDtypeStruct((1, COND_ROWS, ADA_FIRST), F32)] + c_shapes,
        compiler_params=_params("arbitrary", "arbitrary"),
        name="ada_first",
    )(cond, ada_w, ada_b3, *c_ops)
    return outs[0], outs[1:]


def _mod_rows(mod_ref, row, first, count=3):
    d = D_MODEL
    return [mod_ref[0, pl.ds(row, 1), (first + j) * d:(first + j + 1) * d] for j in range(count)]


def _layer_spec(a, layer):
    zeros = (0,) * (a.ndim - 1)
    return pl.BlockSpec((1,) + a.shape[1:], lambda *_: (layer,) + zeros, pipeline_mode=pl.Buffered(1))


def _decay_tables(ctx_len):
    c = RET_CHUNK
    h = np.arange(RET_HEADS, dtype=np.float64)
    lg_f = np.log1p(-np.exp2(-DECAY_EXP_FWD - h))[:, None]
    lg_b = np.log1p(-np.exp2(-DECAY_EXP_BWD - h))[:, None]
    idx = np.arange(c, dtype=np.float64)
    diff = idx[:, None] - idx[None, :]
    mask = (np.where(diff >= 0, np.exp(np.maximum(diff, 0.0)[None] * lg_f[:, :, None]), 0.0)
            + np.where(diff <= 0, np.exp(np.maximum(-diff, 0.0)[None] * lg_b[:, :, None]), 0.0))
    lane = lambda a: jnp.asarray(np.repeat(a[:, :, None], LANES, axis=2), F32)
    cidx = np.arange(ctx_len, dtype=np.float64)
    return dict(
        mask=jnp.asarray(mask, F32),
        qdec_f=lane(np.exp((idx[None] + 1.0) * lg_f)),
        kdec_f=lane(np.exp((c - 1.0 - idx[None]) * lg_f)),
        qdec_b=lane(np.exp((c - idx[None]) * lg_b)),
        kdec_b=lane(np.exp(idx[None] * lg_b)),
        ctxw_f=lane(np.exp((ctx_len - 1.0 - cidx[None]) * lg_f)),
        ctxw_b=lane(np.exp(cidx[None] * lg_b)),
        cdec_f=[float(v) for v in np.exp(c * lg_f[:, 0])],
        cdec_b=[float(v) for v in np.exp(c * lg_b[:, 0])],
    )


def _lanes(tab, n):
    return jnp.concatenate([tab] * (n // LANES), axis=1)


def _head(ref, rows, h, width):
    return ref[0, rows, h * width:(h + 1) * width]


def _scaled_bf16(k, tab):
    return (k.astype(F32) * _lanes(tab, k.shape[1])).astype(BF16)


def _init_state(s_ref, kc_ref, vc_ref, ctxw_ref):
    every = slice(None)
    for h in range(RET_HEADS):
        kw = _scaled_bf16(_head(kc_ref, every, h, RET_QK_DIM), ctxw_ref[h])
        s_ref[h] = _dot_tn(kw, _head(vc_ref, every, h, RET_V_DIM))


def _ctx_kv_kernel(x_ref, mod_ref, gain_ref, wk_ref, wv_ref, k_ref, v_ref):
    d = D_MODEL
    shift, scale = _mod_rows(mod_ref, 2, 0, 2)
    ax = _norm_mod(x_ref[0], gain_ref[0], shift, scale).astype(BF16)
    k_ref[0] = (_dot(ax, wk_ref[...]) * RET_QK_DIM ** -0.5).astype(BF16)
    for j in range(2):
        v_ref[0, :, j * d:(j + 1) * d] = _dot(ax, wv_ref[:, j * d:(j + 1) * d]).astype(BF16)


def _ctx_kv(ctx, mod, gain, w_qkvg, layer):
    bsz, t, d = ctx.shape
    row = lambda b: (b, 0, 0)
    return pl.pallas_call(
        _ctx_kv_kernel,
        grid=(bsz,),
        in_specs=[
            pl.BlockSpec((1, t, d), row),
            _layer_spec(mod, 0),
            _layer_spec(gain, layer),
            pl.BlockSpec((d, d), lambda b: (0, 1), pipeline_mode=pl.Buffered(1)),
            pl.BlockSpec((d, 2 * d), lambda b: (0, 1), pipeline_mode=pl.Buffered(1)),
        ],
        out_specs=[pl.BlockSpec((1, t, d), row), pl.BlockSpec((1, t, 2 * d), row)],
        out_shape=[
            jax.ShapeDtypeStruct((bsz, t, d), BF16),
            jax.ShapeDtypeStruct((bsz, t, 2 * d), BF16),
        ],
        compiler_params=_params("parallel"),
        name="ctx_kv",
    )(ctx, mod, gain, w_qkvg, w_qkvg)


def _rope_tile(rcos_ref, rsin_ref, ccos_ref, csin_ref, tile, tm):
    n_rows = tm // GRID_W
    row0 = tile * n_rows
    def build(r_ref, c_ref):
        rpart = jnp.concatenate(
            [jnp.broadcast_to(r_ref[pl.ds(row0 + r, 1), :], (GRID_W, LANES)) for r in range(n_rows)], axis=0)
        cpart = jnp.concatenate([c_ref[...]] * n_rows, axis=0)
        return jnp.concatenate([rpart, cpart], axis=1)
    return build(rcos_ref, ccos_ref), build(rsin_ref, csin_ref)


def _rope(r, cos, sin):
    outs = []
    for blk in range(r.shape[1] // LANES):
        xb = r[:, blk * LANES:(blk + 1) * LANES]
        tb = (blk % 2) * LANES
        outs.append(xb * cos[:, tb:tb + LANES]
                    + pltpu.roll(xb, LANES // 2, axis=1) * sin[:, tb:tb + LANES])
    return jnp.concatenate(outs, axis=1)


def _qkvg_kernel(x_ref, mod_ref, gain_ref, w_ref, rcos_ref, rsin_ref, ccos_ref, csin_ref,
                 kdec_ref, ctxw_ref, kc_ref, vc_ref, cond_ref, adaw_ref, adab_ref,
                 q_ref, k_ref, v_ref, g_ref, sb_ref, modb_ref, s_ref, *, cdec):
    d = D_MODEL
    tm = x_ref.shape[1]
    step = pl.program_id(1)
    tile = pl.num_programs(1) - 1 - step

    @pl.when(step == 0)
    def _():
        _init_state(s_ref, kc_ref, vc_ref, ctxw_ref)

    shift, scale = _mod_rows(mod_ref, pl.program_id(0), 0, 2)
    ax = _norm_mod(x_ref[0], gain_ref[0], shift, scale).astype(BF16)
    cos, sin = _rope_tile(rcos_ref, rsin_ref, ccos_ref, csin_ref, tile, tm)
    kf = _rope(_dot(ax, w_ref[:, d:2 * d]), cos, sin) * RET_QK_DIM ** -0.5
    k_ref[0] = kf.astype(BF16)
    for j in range(2):
        v_ref[0, :, j * d:(j + 1) * d] = _dot(ax, w_ref[:, (2 + j) * d:(3 + j) * d]).astype(BF16)

    for c in reversed(range(tm // RET_CHUNK)):
        rows = slice(c * RET_CHUNK, (c + 1) * RET_CHUNK)
        for h in range(RET_HEADS):
            s = s_ref[h]
            sb_ref[0, c, h] = s.astype(BF16)
            kd = (kf[rows, h * RET_QK_DIM:(h + 1) * RET_QK_DIM] * _lanes(kdec_ref[h], RET_QK_DIM)).astype(BF16)
            s_ref[h] = s * cdec[h] + _dot_tn(kd, _head(v_ref, rows, h, RET_V_DIM))

    q_ref[0] = _rope(_dot(ax, w_ref[:, 0:d]), cos, sin).astype(BF16)
    for j in range(2):
        g_ref[0, :, j * d:(j + 1) * d] = _dot(ax, w_ref[:, (4 + j) * d:(5 + j) * d]).astype(BF16)
    modb_ref[0] = _ada_rows(cond_ref[...], adaw_ref[0], adab_ref[0])


def _qkvg(x, mod, gain, w, kc, vc, tabs, rope, cond, ada_w, ada_b3, layer, casts, tm=512):
    bsz, t, d = x.shape
    n_t = t // tm
    grid = (bsz, n_t)
    row = lambda b, i: (b, n_t - 1 - i, 0)
    per_b = lambda b, i: (b, 0, 0)
    n_c = tm // RET_CHUNK
    sb_block = (1, n_c, RET_HEADS, RET_QK_DIM, RET_V_DIM)
    depth, _, n_ada = ada_w.shape
    per_layer = n_ada // ADA_BLOCK
    assert depth * per_layer <= bsz * n_t
    def ada_blk(b, i):
        r = jnp.minimum(b * n_t + i, depth * per_layer - 1)
        return r // per_layer, 0, r % per_layer
    c_in, c_out, c_shapes, c_ops = _cast_plan(casts, grid)
    in_specs = [
        pl.BlockSpec((1, tm, d), row),
        _layer_spec(mod, 0),
        _layer_spec(gain, layer),
        _const_spec(w.shape),
        *[_const_spec(r.shape) for r in rope],
        _const_spec(tabs["kdec_b"].shape),
        _const_spec(tabs["ctxw_b"].shape),
        pl.BlockSpec((1,) + kc.shape[1:], per_b),
        pl.BlockSpec((1,) + vc.shape[1:], per_b),
        _const_spec(cond.shape),
        pl.BlockSpec((1, d, ADA_BLOCK), ada_blk),
        pl.BlockSpec((1, 1, ADA_BLOCK), ada_blk),
    ]
    out_specs = [
        pl.BlockSpec((1, tm, d), row),
        pl.BlockSpec((1, tm, d), row),
        pl.BlockSpec((1, tm, 2 * d), row),
        pl.BlockSpec((1, tm, 2 * d), row),
        pl.BlockSpec(sb_block, lambda b, i: (b, n_t - 1 - i, 0, 0, 0)),
        pl.BlockSpec((1, COND_ROWS, ADA_BLOCK), ada_blk),
    ]
    out_shape = [
        jax.ShapeDtypeStruct((bsz, t, d), BF16),
        jax.ShapeDtypeStruct((bsz, t, d), BF16),
        jax.ShapeDtypeStruct((bsz, t, 2 * d), BF16),
        jax.ShapeDtypeStruct((bsz, t, 2 * d), BF16),
        jax.ShapeDtypeStruct((bsz, t // RET_CHUNK) + sb_block[2:], BF16),
        jax.ShapeDtypeStruct((depth, COND_ROWS, n_ada), F32),
    ]
    outs = pl.pallas_call(
        _with_casts(functools.partial(_qkvg_kernel, cdec=tabs["cdec_b"]), len(in_specs), len(out_specs), len(casts)),
        grid=grid,
        in_specs=in_specs + c_in,
        out_specs=out_specs + c_out,
        out_shape=out_shape + c_shapes,
        scratch_shapes=[pltpu.VMEM((RET_HEADS, RET_QK_DIM, RET_V_DIM), F32)],
        compiler_params=_params("arbitrary", "arbitrary"),
        name="qkvg",
    )(x, mod, gain, w, *rope, tabs["kdec_b"], tabs["ctxw_b"], kc, vc, cond, ada_w, ada_b3, *c_ops)
    return outs[:len(out_specs)], outs[len(out_specs):]


def _ret_kernel(q_ref, k_ref, v_ref, g_ref, sb_ref, x_ref, kc_ref, vc_ref, mask_ref, qdf_ref, kdf_ref,
                qdb_ref, ctxw_ref, mod_ref, wo_ref, o_ref, s_ref, y_ref, *, cdec):
    @pl.when(pl.program_id(1) == 0)
    def _():
        _init_state(s_ref, kc_ref, vc_ref, ctxw_ref)

    gate = _mod_rows(mod_ref, pl.program_id(0), 0)[2]
    for c in range(x_ref.shape[1] // RET_CHUNK):
        rows = slice(c * RET_CHUNK, (c + 1) * RET_CHUNK)
        for h in range(RET_HEADS):
            q = _head(q_ref, rows, h, RET_QK_DIM)
            k = _head(k_ref, rows, h, RET_QK_DIM)
            v = _head(v_ref, rows, h, RET_V_DIM)
            p = (_dot_nt(q, k) * mask_ref[h]).astype(BF16)
            s = s_ref[h]
            qf32 = q.astype(F32)
            qf = (qf32 * _lanes(qdf_ref[h], RET_QK_DIM)).astype(BF16)
            qb = (qf32 * _lanes(qdb_ref[h], RET_QK_DIM)).astype(BF16)
            o = _dot(p, v) + _dot(qf, s.astype(BF16)) + _dot(qb, sb_ref[0, c, h])
            o = o * lax.rsqrt(jnp.mean(o * o, axis=-1, keepdims=True) + EPS)
            g = _head(g_ref, rows, h, RET_V_DIM).astype(F32)
            y_ref[rows, h * RET_V_DIM:(h + 1) * RET_V_DIM] = (g * jax.nn.sigmoid(g) * o).astype(BF16)
            s_ref[h] = s * cdec[h] + _dot_tn(_scaled_bf16(k, kdf_ref[h]), v)
    o_ref[0] = x_ref[0] + gate * _dot(y_ref[...], wo_ref[...])


def _retention(x, q, k, v, g, sb, kc, vc, mod, w_o, tabs, layer, blk=512):
    bsz, t, d = x.shape
    row = lambda b, i: (b, i, 0)
    per_b = lambda b, i: (b, 0, 0)
    sb_block = (1, blk // RET_CHUNK) + sb.shape[2:]
    consts = [tabs["mask"], tabs["qdec_f"], tabs["kdec_f"], tabs["qdec_b"], tabs["ctxw_f"]]
    return pl.pallas_call(
        functools.partial(_ret_kernel, cdec=tabs["cdec_f"]),
        grid=(bsz, t // blk),
        in_specs=[
            pl.BlockSpec((1, blk, d), row),
            pl.BlockSpec((1, blk, d), row),
            pl.BlockSpec((1, blk, 2 * d), row),
            pl.BlockSpec((1, blk, 2 * d), row),
            pl.BlockSpec(sb_block, lambda b, i: (b, i, 0, 0, 0)),
            pl.BlockSpec((1, blk, d), row),
            pl.BlockSpec((1,) + kc.shape[1:], per_b),
            pl.BlockSpec((1,) + vc.shape[1:], per_b),
            *[_const_spec(a.shape) for a in consts],
            _layer_spec(mod, layer),
            _const_spec(w_o.shape),
        ],
        out_specs=pl.BlockSpec((1, blk, d), row),
        out_shape=jax.ShapeDtypeStruct((bsz, t, d), F32),
        scratch_shapes=[pltpu.VMEM((RET_HEADS, RET_QK_DIM, RET_V_DIM), F32), pltpu.VMEM((blk, 2 * d), BF16)],
        compiler_params=_params("parallel", "arbitrary"),
        name="ret",
    )(q, k, v, g, sb, x, kc, vc, *consts, mod, w_o)


def _ffn_chunks(f):
    step = 768
    return [(c, min(c + step, f)) for c in range(0, f, step)]


def _ffn_kernel(x_ref, mod_ref, gain_ref, w1_ref, w3_ref, w2_ref, fgain_ref, o_ref, *, final_norm, sub):
    shift, scale, gate = _mod_rows(mod_ref, pl.program_id(0), 3)
    for r0 in range(0, x_ref.shape[1], sub):
        rows = slice(r0, r0 + sub)
        x = x_ref[0, rows]
        fx = _norm_mod(x, gain_ref[0], shift, scale).astype(BF16)
        acc = None
        for c0, c1 in _ffn_chunks(w1_ref.shape[1]):
            a = _dot(fx, w1_ref[:, c0:c1])
            hid = (a * jax.nn.sigmoid(a) * _dot(fx, w3_ref[:, c0:c1])).astype(BF16)
            part = _dot(hid, w2_ref[c0:c1, :])
            acc = part if acc is None else acc + part
        out = x + gate * acc
        if final_norm:
            out = out * lax.rsqrt(jnp.mean(out * out, axis=-1, keepdims=True) + EPS) * fgain_ref[...]
        o_ref[0, rows] = out


def _ffn(x, mod, gain, w1, w3, w2, fgain, layer, final_norm, casts=(), tm=1024, sub=512):
    bsz, t, d = x.shape
    grid = (bsz, t // tm)
    row = lambda b, i: (b, i, 0)
    c_in, c_out, c_shapes, c_ops = _cast_plan(casts, grid)
    in_specs = [
        pl.BlockSpec((1, tm, d), row),
        _layer_spec(mod, layer),
        _layer_spec(gain, layer),
        _const_spec(w1.shape),
        _const_spec(w3.shape),
        _const_spec(w2.shape),
        _const_spec(fgain.shape),
    ]
    outs = pl.pallas_call(
        _with_casts(functools.partial(_ffn_kernel, final_norm=final_norm, sub=sub), len(in_specs), 1, len(casts)),
        grid=grid,
        in_specs=in_specs + c_in,
        out_specs=[pl.BlockSpec((1, tm, d), row)] + c_out,
        out_shape=[jax.ShapeDtypeStruct((bsz, t, d), F32)] + c_shapes,
        compiler_params=_params("arbitrary", "arbitrary"),
        name="ffn_final" if final_norm else "ffn",
    )(x, mod, gain, w1, w3, w2, fgain, *c_ops)
    return outs[0], outs[1:]


def _conv_kernel(x_ref, prev_ref, next_ref, mod_ref, gain_ref, win_ref, cw_ref, wout_ref, o_ref, u_ref):
    d = D_MODEL
    tm = x_ref.shape[1]
    i, last = pl.program_id(1), pl.num_programs(1) - 1
    shift, scale, gate = _mod_rows(mod_ref, pl.program_id(0), 0)
    x = x_ref[0]
    halo = jnp.concatenate([prev_ref[0], next_ref[0]], axis=0)
    ax = jnp.concatenate([_norm_mod(x, gain_ref[0], shift, scale),
                          _norm_mod(halo, gain_ref[0], shift, scale)], axis=0).astype(BF16)
    u = _dot(ax, win_ref[:, d:2 * d]) * _dot(ax, win_ref[:, 2 * d:3 * d])
    u_ref[0:HALO] = u[tm:tm + HALO] * jnp.where(i == 0, 0.0, 1.0)
    u_ref[HALO:HALO + tm] = u[0:tm]
    u_ref[HALO + tm:] = u[tm + HALO:] * jnp.where(i == last, 0.0, 1.0)
    cw = cw_ref[0]
    conv = (u_ref[pl.ds(HALO - 1, tm)] * cw[0:1] + u[0:tm] * cw[1:2]
            + u_ref[pl.ds(HALO + 1, tm)] * cw[2:3])
    b_gate = _dot(ax[0:tm], win_ref[:, 0:d])
    o_ref[0] = x + gate * _dot((b_gate * conv).astype(BF16), wout_ref[...])


def _conv(x, mod, gain, w_in, conv_w, w_out, layer, tm=1024):
    bsz, t, d = x.shape
    per = tm // HALO
    n_halo = t // HALO
    row = lambda b, i: (b, i, 0)
    return pl.pallas_call(
        _conv_kernel,
        grid=(bsz, t // tm),
        in_specs=[
            pl.BlockSpec((1, tm, d), row),
            pl.BlockSpec((1, HALO, d), lambda b, i: (b, jnp.maximum(i * per - 1, 0), 0)),
            pl.BlockSpec((1, HALO, d), lambda b, i: (b, jnp.minimum((i + 1) * per, n_halo - 1), 0)),
            _layer_spec(mod, layer),
            _layer_spec(gain, layer),
            _const_spec(w_in.shape),
            _layer_spec(conv_w, 0),
            _const_spec(w_out.shape),
        ],
        out_specs=pl.BlockSpec((1, tm, d), row),
        out_shape=jax.ShapeDtypeStruct((bsz, t, d), F32),
        scratch_shapes=[pltpu.VMEM((tm + 2 * HALO, d), F32)],
        compiler_params=_params("parallel", "parallel"),
        name="conv",
    )(x, x, x, mod, gain, w_in, conv_w, w_out)


def _rope_tables(t):
    n = LANES // 2
    freqs = ROPE_BASE ** (-np.arange(n, dtype=np.float64) / n)
    def tabs(count):
        ang = np.arange(count, dtype=np.float64)[:, None] * freqs[None, :]
        return (jnp.asarray(np.concatenate([np.cos(ang), np.cos(ang)], axis=1), F32),
                jnp.asarray(np.concatenate([-np.sin(ang), np.sin(ang)], axis=1), F32))
    rcos, rsin = tabs(t // GRID_W)
    ccos, csin = tabs(GRID_W)
    return rcos, rsin, ccos, csin


def kernel(x, c, ctx, c_ctx, ada_w, ada_b, norm_mix, norm_ffn, ret_w_qkvg, ret_w_o, conv_w_in, conv_w,
           conv_w_out, ffn_w1, ffn_w3, ffn_w2, final_norm):
    bsz, t, d = x.shape
    depth = ada_w.shape[0]
    half = COND_ROWS // 2
    assert bsz + 1 <= half
    rows = jnp.concatenate([c, c_ctx[None], jnp.zeros((half - bsz - 1, d), F32)], axis=0)
    cond = jnp.concatenate([rows, rows], axis=0)
    ada_b3 = ada_b.reshape(depth, 1, -1)
    gain_mix = norm_mix.reshape(depth, 1, d)
    gain_ffn = norm_ffn.reshape(depth, 1, d)
    fgain = final_norm.reshape(1, d)
    tabs = _decay_tables(ctx.shape[1])

    mod0, (w_qkvg,) = _ada_first(cond, ada_w, ada_b3, 0, casts=[(ret_w_qkvg, 0)])
    kc, vc = _ctx_kv(ctx, mod0, gain_mix, w_qkvg, 0)
    (q, k, v, g, sb, mod), (w_o, w1, w3, w2) = _qkvg(
        x, mod0, gain_mix, w_qkvg, kc, vc, tabs, _rope_tables(t), cond, ada_w, ada_b3, 0,
        casts=[(ret_w_o, 0), (ffn_w1, 0), (ffn_w3, 0), (ffn_w2, 0)])
    hx = _retention(x, q, k, v, g, sb, kc, vc, mod, w_o, tabs, 0)
    hx, (w_in, w_out, w1, w3, w2) = _ffn(
        hx, mod, gain_ffn, w1, w3, w2, fgain, 0, final_norm=False,
        casts=[(conv_w_in, 0), (conv_w_out, 0), (ffn_w1, 1), (ffn_w3, 1), (ffn_w2, 1)])

    hx = _conv(hx, mod, gain_mix, w_in, conv_w, w_out, 1)
    return _ffn(hx, mod, gain_ffn, w1, w3, w2, fgain, 1, final_norm=True)[0]
```
